```python
import math
import jax, jax.numpy as jnp
from jax import lax
import numpy as np

D_MODEL = 1024
BATCH = 16
SEQ = 2048
DEPTH = 4

MIX_WIDTH = D_MODEL
GROUP_WIDTH = MIX_WIDTH // 4
MIX_HEAD_DIM = 64
MIX_HEADS = MIX_WIDTH // MIX_HEAD_DIM

SC_WIDTH = GROUP_WIDTH
SC_KERNEL = 3

MLA_HEADS = 4
MLA_NOPE = 64
MLA_ROPE = 32
MLA_V = GROUP_WIDTH // MLA_HEADS
MLA_Q_RANK = 256
MLA_KV_RANK = 128
ROPE_THETA = 10000.0
Q_BLOCK = 128

CF_WIDTH = GROUP_WIDTH
CF_KERNEL = 31

SWA_Q_HEADS = 4
SWA_KV_HEADS = 2
SWA_HEAD_DIM = GROUP_WIDTH // SWA_Q_HEADS
WINDOW = 128

PEER_HEADS = 8
PEER_NKEYS = 128
PEER_EXPERTS = PEER_NKEYS * PEER_NKEYS
PEER_QDIM = 256
PEER_HALF = PEER_QDIM // 2
PEER_TOPK = 16
PEER_CHUNK = 128

ALPHA = (2 * DEPTH) ** 0.25
BETA = (8 * DEPTH) ** -0.25
NORM_EPS = 1e-5

SC_COLS = 3 * SC_WIDTH
MLA_COLS = MLA_Q_RANK + MLA_KV_RANK + MLA_ROPE
CF_COLS = 2 * CF_WIDTH
SWA_COLS = (SWA_Q_HEADS + 2 * SWA_KV_HEADS) * SWA_HEAD_DIM
IN_COLS = SC_COLS + MLA_COLS + CF_COLS + SWA_COLS

kernel_name = "hybrid_parallel_groups_peer_deepnorm"


def _split_points(sizes):
    pts, acc = [], 0
    for s in sizes[:-1]:
        acc += s
        pts.append(acc)
    return pts


def layer_norm(x, g, b):
    xf = x.astype(jnp.float32)
    mu = jnp.mean(xf, axis=-1, keepdims=True)
    var = jnp.mean(jnp.square(xf - mu), axis=-1, keepdims=True)
    y = (xf - mu) * lax.rsqrt(var + NORM_EPS)
    return (y * g.astype(jnp.float32) + b.astype(jnp.float32)).astype(x.dtype)


def rms_norm(x, g):
    xf = x.astype(jnp.float32)
    y = xf * lax.rsqrt(jnp.mean(xf * xf, axis=-1, keepdims=True) + NORM_EPS)
    return (y * g.astype(jnp.float32)).astype(x.dtype)


def causal_depthwise_conv(x, w):
    k = w.shape[0]
    xp = jnp.pad(x, ((0, 0), (k - 1, 0), (0, 0)))
    return lax.conv_general_dilated(
        xp, w[:, None, :], window_strides=(1,), padding='VALID',
        dimension_numbers=('NWC', 'WIO', 'NWC'), feature_group_count=x.shape[-1])


def rope(x, positions):
    d = x.shape[-1]
    half = d // 2
    inv = ROPE_THETA ** (-jnp.arange(half, dtype=jnp.float32) / half)
    ang = positions.astype(jnp.float32)[..., None] * inv
    ang = ang.reshape(ang.shape[:2] + (1,) * (x.ndim - 3) + (half,))
    cos, sin = jnp.cos(ang), jnp.sin(ang)
    xf = x.astype(jnp.float32)
    x1, x2 = xf[..., :half], xf[..., half:]
    return jnp.concatenate([x1 * cos - x2 * sin, x2 * cos + x1 * sin], axis=-1).astype(x.dtype)


def alibi_slopes(n):
    return 2.0 ** (-8.0 * jnp.arange(1, n + 1, dtype=jnp.float32) / n)


def short_conv_mixer(gb, gc, h, conv_w):
    return gb * causal_depthwise_conv(gc * h, conv_w)


def mla_attention(c_q, c_kv, k_rope_in, positions, q_norm_g, kv_norm_g, w_uq, w_uk, w_uv):
    b, s, _ = c_q.shape
    q = (rms_norm(c_q, q_norm_g) @ w_uq).reshape(b, s, MLA_HEADS, MLA_NOPE + MLA_ROPE)
    q_nope = q[..., :MLA_NOPE]
    q_pe = rope(q[..., MLA_NOPE:], positions)
    ckv = rms_norm(c_kv, kv_norm_g)
    k_nope = (ckv @ w_uk).reshape(b, s, MLA_HEADS, MLA_NOPE)
    v = (ckv @ w_uv).reshape(b, s, MLA_HEADS, MLA_V)
    k_pe = rope(k_rope_in, positions)
    nq = s // Q_BLOCK
    qn = q_nope.reshape(b, nq, Q_BLOCK, MLA_HEADS, MLA_NOPE).swapaxes(0, 1)
    qp = q_pe.reshape(b, nq, Q_BLOCK, MLA_HEADS, MLA_ROPE).swapaxes(0, 1)
    key_idx = jnp.arange(s)
    scale = (MLA_NOPE + MLA_ROPE) ** -0.5

    def one_block(args):
        blk, qn_b, qp_b = args
        sc = (jnp.einsum('bqhd,bkhd->bhqk', qn_b, k_nope)
              + jnp.einsum('bqhd,bkd->bhqk', qp_b, k_pe)).astype(jnp.float32) * scale
        q_idx = blk * Q_BLOCK + jnp.arange(Q_BLOCK)
        causal = key_idx[None, :] <= q_idx[:, None]
        sc = jnp.where(causal, sc, -jnp.inf)
        p = jax.nn.softmax(sc, axis=-1).astype(v.dtype)
        return jnp.einsum('bhqk,bkhd->bqhd', p, v)

    out = lax.map(one_block, (jnp.arange(nq), qn, qp))
    return out.swapaxes(0, 1).reshape(b, s, MLA_HEADS * MLA_V)


def conformer_conv(a, gate, dw_w, dw_b, ln_g, ln_b):
    u = a * jax.nn.sigmoid(gate)
    u = causal_depthwise_conv(u, dw_w) + dw_b
    u = layer_norm(u, ln_g, ln_b)
    return jax.nn.silu(u)


def swa_sink_attention(q, k, v, sinks, slopes):
    b, s, _ = q.shape
    g = SWA_Q_HEADS // SWA_KV_HEADS
    nb = s // WINDOW
    qb = q.reshape(b, nb, WINDOW, SWA_KV_HEADS, g, SWA_HEAD_DIM)
    kb = k.reshape(b, nb, WINDOW, SWA_KV_HEADS, SWA_HEAD_DIM)
    vb = v.reshape(b, nb, WINDOW, SWA_KV_HEADS, SWA_HEAD_DIM)

    def with_prev(t):
        prev = jnp.pad(t, ((0, 0), (1, 0), (0, 0), (0, 0), (0, 0)))[:, :-1]
        return jnp.concatenate([prev, t], axis=2)

    kk, vv = with_prev(kb), with_prev(vb)
    sc = jnp.einsum('bnqkgd,bnskd->bnkgqs', qb, kk).astype(jnp.float32) * (SWA_HEAD_DIM ** -0.5)
    qi = jnp.arange(WINDOW)[:, None]
    kj = jnp.arange(2 * WINDOW)[None, :]
    dist = qi + WINDOW - kj
    blk = jnp.arange(nb)[:, None, None]
    valid = ((dist >= 0) & (dist < WINDOW))[None] & ((blk > 0) | (kj[None] >= WINDOW))
    sc = sc - slopes.reshape(SWA_KV_HEADS, g, 1, 1) * dist.astype(jnp.float32)
    sc = jnp.where(valid[None, :, None, None], sc, -jnp.inf)
    sink = jnp.broadcast_to(sinks.astype(jnp.float32).reshape(SWA_KV_HEADS, g, 1, 1), sc.shape[:-1] + (1,))
    p = jax.nn.softmax(jnp.concatenate([sc, sink], axis=-1), axis=-1)[..., :-1].astype(v.dtype)
    o = jnp.einsum('bnkgqs,bnskd->bnqkgd', p, vv)
    return o.reshape(b, s, SWA_Q_HEADS * SWA_HEAD_DIM)


def peer_ffn(x, w_q, sub_keys, u_tab, v_tab):
    b, s, d = x.shape
    xt = x.reshape((b * s) // PEER_CHUNK, PEER_CHUNK, d)

    def chunk(xc):
        c = xc.shape[0]
        q = (xc @ w_q).reshape(c, PEER_HEADS, 2, PEER_HALF)
        sc = jnp.einsum('chpd,hpnd->chpn', q, sub_keys).astype(jnp.float32)
        top_s, top_i = lax.top_k(sc, PEER_TOPK)
        cand_s = top_s[:, :, 0, :, None] + top_s[:, :, 1, None, :]
        cand_i = top_i[:, :, 0, :, None] * PEER_NKEYS + top_i[:, :, 1, None, :]
        best_s, best_j = lax.top_k(cand_s.reshape(c, PEER_HEADS, PEER_TOPK * PEER_TOPK), PEER_TOPK)
        idx = jnp.take_along_axis(cand_i.reshape(c, PEER_HEADS, PEER_TOPK * PEER_TOPK), best_j, axis=-1)
        gate = jax.nn.softmax(best_s, axis=-1).astype(xc.dtype)
        u = u_tab[idx]
        act = jax.nn.gelu(jnp.einsum('chkd,cd->chk', u, xc), approximate=False)
        vsel = v_tab[idx]
        return jnp.einsum('chk,chkd->cd', gate * act, vsel)

    return lax.map(chunk, xt).reshape(b, s, d)


def setup_inputs(seed: int = 0) -> dict:
    key = jax.random.key(seed)
    ks = jax.random.split(key, 32)
    L = DEPTH

    def nrm(k, shape, scale):
        return jax.random.normal(k, shape, jnp.float32) * scale

    x = nrm(ks[0], (BATCH, SEQ, D_MODEL), 1.0)
    offs = jax.random.randint(ks[1], (BATCH, 1), 0, 4096, dtype=jnp.int32)
    positions = (offs + jnp.arange(SEQ, dtype=jnp.int32)[None, :]).astype(jnp.int32)
    return {
        "x": x,
        "positions": positions,
        "w_in": nrm(ks[2], (L, D_MODEL, IN_COLS), D_MODEL ** -0.5),
        "sc_conv_w": nrm(ks[3], (L, SC_KERNEL, SC_WIDTH), SC_KERNEL ** -0.5),
        "mla_q_norm": 1.0 + nrm(ks[4], (L, MLA_Q_RANK), 0.02),
        "mla_kv_norm": 1.0 + nrm(ks[5], (L, MLA_KV_RANK), 0.02),
        "mla_w_uq": nrm(ks[6], (L, MLA_Q_RANK, MLA_HEADS * (MLA_NOPE + MLA_ROPE)), MLA_Q_RANK ** -0.5),
        "mla_w_uk": nrm(ks[7], (L, MLA_KV_RANK, MLA_HEADS * MLA_NOPE), MLA_KV_RANK ** -0.5),
        "mla_w_uv": nrm(ks[8], (L, MLA_KV_RANK, MLA_HEADS * MLA_V), MLA_KV_RANK ** -0.5),
        "cf_dw_w": nrm(ks[9], (L, CF_KERNEL, CF_WIDTH), CF_KERNEL ** -0.5),
        "cf_dw_b": nrm(ks[10], (L, CF_WIDTH), 0.02),
        "cf_ln_g": 1.0 + nrm(ks[11], (L, CF_WIDTH), 0.02),
        "cf_ln_b": nrm(ks[12], (L, CF_WIDTH), 0.02),
        "swa_sinks": nrm(ks[13], (L, SWA_Q_HEADS), 0.5),
        "mix_norm_g": 1.0 + nrm(ks[14], (L, MIX_WIDTH), 0.02),
        "w_out": nrm(ks[15], (L, MIX_WIDTH, D_MODEL), BETA * MIX_WIDTH ** -0.5),
        "ln1_g": 1.0 + nrm(ks[16], (L, D_MODEL), 0.02),
        "ln1_b": nrm(ks[17], (L, D_MODEL), 0.02),
        "peer_w_q": nrm(ks[18], (L, D_MODEL, PEER_HEADS * PEER_QDIM), D_MODEL ** -0.5),
        "peer_sub_keys": nrm(ks[19], (L, PEER_HEADS, 2, PEER_NKEYS, PEER_HALF), PEER_HALF ** -0.5),
        "peer_u": nrm(ks[20], (L, PEER_EXPERTS, D_MODEL), D_MODEL ** -0.5),
        "peer_v": nrm(ks[21], (L, PEER_EXPERTS, D_MODEL), BETA * PEER_HEADS ** -0.5),
        "ln2_g": 1.0 + nrm(ks[22], (L, D_MODEL), 0.02),
        "ln2_b": nrm(ks[23], (L, D_MODEL), 0.02),
    }


def reference(x, positions, w_in, sc_conv_w, mla_q_norm, mla_kv_norm, mla_w_uq, mla_w_uk, mla_w_uv,
              cf_dw_w, cf_dw_b, cf_ln_g, cf_ln_b, swa_sinks, mix_norm_g, w_out, ln1_g, ln1_b,
              peer_w_q, peer_sub_keys, peer_u, peer_v, ln2_g, ln2_b):
    b, s, _ = x.shape
    slopes = alibi_slopes(SWA_Q_HEADS)
    split_sizes = [SC_WIDTH, SC_WIDTH, SC_WIDTH,
                   MLA_Q_RANK, MLA_KV_RANK, MLA_ROPE,
                   CF_WIDTH, CF_WIDTH,
                   SWA_Q_HEADS * SWA_HEAD_DIM, SWA_KV_HEADS * SWA_HEAD_DIM, SWA_KV_HEADS * SWA_HEAD_DIM]
    pts = _split_points(split_sizes)
    for l in range(DEPTH):
        proj = x @ w_in[l]
        (sc_b, sc_c, sc_h, m_cq, m_ckv, m_kr, cf_a, cf_g, sw_q, sw_k, sw_v) = jnp.split(proj, pts, axis=-1)
        y_sc = short_conv_mixer(sc_b, sc_c, sc_h, sc_conv_w[l])
        y_mla = mla_attention(m_cq, m_ckv, m_kr, positions, mla_q_norm[l], mla_kv_norm[l],
                              mla_w_uq[l], mla_w_uk[l], mla_w_uv[l])
        y_cf = conformer_conv(cf_a, cf_g, cf_dw_w[l], cf_dw_b[l], cf_ln_g[l], cf_ln_b[l])
        y_sw = swa_sink_attention(sw_q, sw_k, sw_v, swa_sinks[l], slopes)
        mix = jnp.concatenate([y_sc, y_mla, y_cf, y_sw], axis=-1)
        mix = rms_norm(mix.reshape(b, s, MIX_HEADS, MIX_HEAD_DIM),
                       mix_norm_g[l].reshape(MIX_HEADS, MIX_HEAD_DIM)).reshape(b, s, MIX_WIDTH)
        h = layer_norm(ALPHA * x + mix @ w_out[l], ln1_g[l], ln1_b[l])
        x = layer_norm(ALPHA * h + peer_ffn(h, peer_w_q[l], peer_sub_keys[l], peer_u[l], peer_v[l]),
                       ln2_g[l], ln2_b[l])
    return x
```

```python
import functools
import math

import numpy as np
import jax
import jax.numpy as jnp
from jax import lax
from jax.experimental import pallas as pl
from jax.experimental.pallas import tpu as pltpu

f32 = jnp.float32
bf16 = jnp.bfloat16

D_MODEL = 1024
DEPTH = 4
GROUP = 256
HEAD = 64
SC_KERNEL = 3
MLA_HEADS = 4
MLA_NOPE = 64
MLA_ROPE = 32
MLA_Q_RANK = 256
MLA_KV_RANK = 128
ROPE_THETA = 10000.0
Q_BLOCK = 128
CF_KERNEL = 31
SWA_Q_HEADS = 4
SWA_KV_HEADS = 2
WINDOW = 128
PEER_HEADS = 8
PEER_NKEYS = 128
PEER_TOPK = 16
PEER_HALF = 128
PEER_SLOTS = PEER_HEADS * PEER_TOPK
ALPHA = (2 * DEPTH) ** 0.25
NORM_EPS = 1e-5

LANES = 128
SUBLANES = 8
VMEM_LIMIT = 48 * 1024 * 1024

SEG_SC = 3 * GROUP
SEG_CF = 2 * GROUP
SEG_MC = MLA_Q_RANK + MLA_KV_RANK
SEG_KR = 2 * LANES
SEG_SWQ = SWA_Q_HEADS * LANES
SEG_SWK = SWA_KV_HEADS * LANES
SEG_SWV = 2 * SWA_KV_HEADS * LANES
SEGS = (SEG_SC, SEG_CF, SEG_MC, SEG_KR, SEG_SWQ, SEG_SWK, SEG_SWV)
SEG_OFF = tuple(int(v) for v in np.cumsum((0,) + SEGS))
IN_COLS_PADDED = SEG_OFF[-1]
ROPE_LANE0 = MLA_NOPE


def _inproj_column_map():
    o_sc, o_cq, o_ckv, o_kr, o_cf = 0, 768, 1024, 1152, 1184
    o_swq, o_swk, o_swv = 1696, 1952, 2080
    src = np.zeros(IN_COLS_PADDED, np.int32)
    sgn = np.zeros(IN_COLS_PADDED, np.float32)

    def put(dst, cols, sign=1.0):
        src[dst:dst + len(cols)] = cols
        sgn[dst:dst + len(cols)] = sign

    half = MLA_ROPE // 2
    put(SEG_OFF[0], o_sc + np.arange(SEG_SC))
    put(SEG_OFF[1], o_cf + np.arange(SEG_CF))
    put(SEG_OFF[2], o_cq + np.arange(SEG_MC))
    x1 = o_kr + np.arange(half)
    x2 = o_kr + half + np.arange(half)
    put(SEG_OFF[3] + ROPE_LANE0, x1)
    put(SEG_OFF[3] + ROPE_LANE0 + half, x2)
    put(SEG_OFF[3] + LANES + ROPE_LANE0, x2, -1.0)
    put(SEG_OFF[3] + LANES + ROPE_LANE0 + half, x1)
    for h in range(SWA_Q_HEADS):
        put(SEG_OFF[4] + h * LANES, o_swq + h * HEAD + np.arange(HEAD))
    for g in range(SWA_KV_HEADS):
        put(SEG_OFF[5] + g * LANES, o_swk + g * HEAD + np.arange(HEAD))
        put(SEG_OFF[6] + (2 * g) * LANES, o_swv + g * HEAD + np.arange(HEAD))
        put(SEG_OFF[6] + (2 * g + 1) * LANES + HEAD, o_swv + g * HEAD + np.arange(HEAD))
    return src, sgn


def _mla_weight_maps():
    half = MLA_ROPE // 2
    qd = MLA_NOPE + MLA_ROPE
    n = MLA_HEADS * LANES
    q_src = np.zeros(n, np.int32); q_sgn = np.zeros(n, np.float32)
    r_src = np.zeros(n, np.int32); r_sgn = np.zeros(n, np.float32)
    k_src = np.zeros(n, np.int32); k_sgn = np.zeros(n, np.float32)
    v_src = np.zeros(n, np.int32); v_sgn = np.zeros(n, np.float32)
    for h in range(MLA_HEADS):
        b = h * LANES
        q_src[b:b + qd] = h * qd + np.arange(qd); q_sgn[b:b + qd] = 1.0
        x1 = h * qd + MLA_NOPE + np.arange(half)
        x2 = x1 + half
        r_src[b + ROPE_LANE0:b + ROPE_LANE0 + half] = x2; r_sgn[b + ROPE_LANE0:b + ROPE_LANE0 + half] = -1.0
        r_src[b + ROPE_LANE0 + half:b + ROPE_LANE0 + 2 * half] = x1; r_sgn[b + ROPE_LANE0 + half:b + ROPE_LANE0 + 2 * half] = 1.0
        k_src[b:b + MLA_NOPE] = h * MLA_NOPE + np.arange(MLA_NOPE); k_sgn[b:b + MLA_NOPE] = 1.0
        lo = b + (HEAD if h % 2 else 0)
        v_src[lo:lo + HEAD] = h * HEAD + np.arange(HEAD); v_sgn[lo:lo + HEAD] = 1.0
    return (q_src, q_sgn), (r_src, r_sgn), (k_src, k_sgn), (v_src, v_sgn)


def _relayout_cols(w, src_sgn):
    src, sgn = src_sgn
    return jnp.take(w, jnp.asarray(src), axis=1) * jnp.asarray(sgn)[None, :]


def _layer_norm(v, g, b):
    mu = jnp.mean(v, axis=-1, keepdims=True)
    c = v - mu
    var = jnp.mean(c * c, axis=-1, keepdims=True)
    return c * lax.rsqrt(var + NORM_EPS) * g + b


def _cparams(*sem):
    return pltpu.CompilerParams(dimension_semantics=sem, vmem_limit_bytes=VMEM_LIMIT)


def _rope_kernel(pos_ref, inv_ref, cos_ref, sin_ref):
    ang = pos_ref[...].astype(f32) * inv_ref[...]
    cos_ref[...] = jnp.cos(ang)
    sin_ref[...] = jnp.sin(ang)


def rope_tables(positions, tm=1024):
    t = positions.size
    half = MLA_ROPE // 2
    inv = ROPE_THETA ** (-jnp.arange(half, dtype=f32) / half)
    inv_row = jnp.zeros((1, LANES), f32)
    inv_row = inv_row.at[0, ROPE_LANE0:ROPE_LANE0 + half].set(inv)
    inv_row = inv_row.at[0, ROPE_LANE0 + half:ROPE_LANE0 + 2 * half].set(inv)
    tm = min(tm, t)
    return pl.pallas_call(
        _rope_kernel,
        grid=(t // tm,),
        in_specs=[pl.BlockSpec((tm, 1), lambda i: (i, 0)), pl.BlockSpec((1, LANES), lambda i: (0, 0))],
        out_specs=[pl.BlockSpec((tm, LANES), lambda i: (i, 0))] * 2,
        out_shape=[jax.ShapeDtypeStruct((t, LANES), f32)] * 2,
        compiler_params=_cparams("arbitrary"),
        name="rope_tables",
    )(positions.reshape(t, 1), inv_row)


def _inproj_kernel(x_ref, w_ref, sc_ref, cf_ref, mc_ref, kr_ref, q_ref, k_ref, v_ref):
    xb = x_ref[...].astype(bf16)
    outs = (sc_ref, cf_ref, mc_ref, kr_ref, q_ref, k_ref, v_ref)
    for j, o_ref in enumerate(outs):
        acc = jnp.dot(xb, w_ref[:, SEG_OFF[j]:SEG_OFF[j + 1]], preferred_element_type=f32)
        o_ref[...] = acc.astype(o_ref.dtype)


def inproj(x2d, w_cat, tm=512):
    t = x2d.shape[0]
    tm = min(tm, t)
    dts = (f32, f32, f32, f32, bf16, bf16, bf16)
    return pl.pallas_call(
        _inproj_kernel,
        grid=(t // tm,),
        in_specs=[pl.BlockSpec((tm, D_MODEL), lambda i: (i, 0)),
                  pl.BlockSpec((D_MODEL, IN_COLS_PADDED), lambda i: (0, 0))],
        out_specs=[pl.BlockSpec((tm, w), lambda i: (i, 0)) for w in SEGS],
        out_shape=[jax.ShapeDtypeStruct((t, w), d) for w, d in zip(SEGS, dts)],
        compiler_params=_cparams("arbitrary"),
        name="inproj",
    )(x2d, w_cat)


CONV_ROWS = 64
SC_HALO = 8
CF_HALO = 32


def _local_mix_kernel(sc_ref, sch_ref, cf_ref, cfh_ref, scw_ref, cfw_ref, cfb_ref, lng_ref, lnb_ref,
                      ysc_ref, ycf_ref, pad_sc, pad_cf):
    ts = sc_ref.shape[0]
    first = pl.program_id(1) == 0
    g = GROUP

    pad_sc[SC_HALO:SC_HALO + ts, :] = sc_ref[:, g:2 * g] * sc_ref[:, 2 * g:3 * g]
    halo = sch_ref[:, g:2 * g] * sch_ref[:, 2 * g:3 * g]
    pad_sc[0:SC_HALO, :] = jnp.where(first, 0.0, halo)
    for c in range(ts // CONV_ROWS):
        r0 = c * CONV_ROWS
        acc = jnp.zeros((CONV_ROWS, g), f32)
        for k in range(SC_KERNEL):
            o = SC_HALO + r0 - (SC_KERNEL - 1) + k
            acc = acc + scw_ref[k:k + 1, :] * pad_sc[o:o + CONV_ROWS, :]
        ysc_ref[r0:r0 + CONV_ROWS, :] = sc_ref[r0:r0 + CONV_ROWS, 0:g] * acc

    pad_cf[CF_HALO:CF_HALO + ts, :] = cf_ref[:, 0:g] * jax.nn.sigmoid(cf_ref[:, g:2 * g])
    halo = cfh_ref[:, 0:g] * jax.nn.sigmoid(cfh_ref[:, g:2 * g])
    pad_cf[0:CF_HALO, :] = jnp.where(first, 0.0, halo)
    for c in range(ts // CONV_ROWS):
        r0 = c * CONV_ROWS
        acc = jnp.zeros((CONV_ROWS, g), f32) + cfb_ref[...]
        for k in range(CF_KERNEL):
            o = CF_HALO + r0 - (CF_KERNEL - 1) + k
            acc = acc + cfw_ref[k:k + 1, :] * pad_cf[o:o + CONV_ROWS, :]
        u = _layer_norm(acc, lng_ref[...], lnb_ref[...])
        ycf_ref[r0:r0 + CONV_ROWS, :] = u * jax.nn.sigmoid(u)


def local_mix(sc, cf, sc_w, cf_w, cf_b, ln_g, ln_b, batch, seq, ts=512):
    t = sc.shape[0]
    ts = min(ts, seq)
    nsb = seq // ts
    main = lambda b, n: (b * nsb + n, 0)

    def halo(rows):
        per = ts // rows
        return lambda b, n: (jnp.maximum((b * nsb + n) * per - 1, 0), 0)

    row = lambda w: pl.BlockSpec((1, w), lambda b, n: (0, 0))
    return pl.pallas_call(
        _local_mix_kernel,
        grid=(batch, nsb),
        in_specs=[pl.BlockSpec((ts, SEG_SC), main), pl.BlockSpec((SC_HALO, SEG_SC), halo(SC_HALO)),
                  pl.BlockSpec((ts, SEG_CF), main), pl.BlockSpec((CF_HALO, SEG_CF), halo(CF_HALO)),
                  pl.BlockSpec((SC_KERNEL, GROUP), lambda b, n: (0, 0)),
                  pl.BlockSpec((CF_KERNEL, GROUP), lambda b, n: (0, 0)),
                  row(GROUP), row(GROUP), row(GROUP)],
        out_specs=[pl.BlockSpec((ts, GROUP), main)] * 2,
        out_shape=[jax.ShapeDtypeStruct((t, GROUP), f32)] * 2,
        scratch_shapes=[pltpu.VMEM((SC_HALO + ts, GROUP), f32), pltpu.VMEM((CF_HALO + ts, GROUP), f32)],
        compiler_params=_cparams("arbitrary", "arbitrary"),
        name="local_mix",
    )(sc, sc, cf, cf, sc_w, cf_w, cf_b.reshape(1, GROUP), ln_g.reshape(1, GROUP), ln_b.reshape(1, GROUP))


def _swa_kernel(sinks_ref, q_ref, kc_ref, kp_ref, vc_ref, vp_ref, o_ref):
    n = pl.program_id(1)
    w = WINDOW
    kk = jnp.concatenate([kp_ref[...], kc_ref[...]], axis=0)
    vv = jnp.concatenate([vp_ref[...], vc_ref[...]], axis=0)
    qi = lax.broadcasted_iota(jnp.int32, (w, 2 * w), 0)
    kj = lax.broadcasted_iota(jnp.int32, (w, 2 * w), 1)
    dist = qi + w - kj
    valid = (dist >= 0) & (dist < w) & ((n > 0) | (kj >= w))
    distf = dist.astype(f32)
    group = SWA_Q_HEADS // SWA_KV_HEADS
    for g in range(SWA_KV_HEADS):
        acc = jnp.zeros((w, LANES), f32)
        for gi in range(group):
            hq = g * group + gi
            slope = 2.0 ** (-8.0 * (hq + 1) / SWA_Q_HEADS)
            s = lax.dot_general(q_ref[:, hq * LANES:(hq + 1) * LANES], kk[:, g * LANES:(g + 1) * LANES],
                                (((1,), (1,)), ((), ())), preferred_element_type=f32)
            s = s * (HEAD ** -0.5) - slope * distf
            s = jnp.where(valid, s, -jnp.inf)
            sink = sinks_ref[hq]
            m = jnp.maximum(jnp.max(s, axis=-1, keepdims=True), sink)
            e = jnp.exp(s - m)
            den = jnp.sum(e, axis=-1, keepdims=True) + jnp.exp(sink - m)
            p = (e / den).astype(bf16)
            blk = 2 * g + gi
            acc = acc + jnp.dot(p, vv[:, blk * LANES:(blk + 1) * LANES], preferred_element_type=f32)
        o_ref[:, g * LANES:(g + 1) * LANES] = acc


def swa(q, k, v, sinks, batch, seq):
    t = q.shape[0]
    nb = seq // WINDOW
    cur = lambda b, n: (b * nb + n, 0)
    prev = lambda b, n: (b * nb + jnp.maximum(n - 1, 0), 0)
    return pl.pallas_call(
        _swa_kernel,
        grid=(batch, nb),
        in_specs=[pl.BlockSpec(memory_space=pltpu.SMEM),
                  pl.BlockSpec((WINDOW, SEG_SWQ), cur),
                  pl.BlockSpec((WINDOW, SEG_SWK), cur), pl.BlockSpec((WINDOW, SEG_SWK), prev),
                  pl.BlockSpec((WINDOW, SEG_SWV), cur), pl.BlockSpec((WINDOW, SEG_SWV), prev)],
        out_specs=pl.BlockSpec((WINDOW, GROUP), cur),
        out_shape=jax.ShapeDtypeStruct((t, GROUP), f32),
        compiler_params=_cparams("arbitrary", "arbitrary"),
        name="swa",
    )(sinks, q, k, k, v, v)


def _mla_prep_kernel(mc_ref, kr_ref, cos_ref, sin_ref, qg_ref, kg_ref, wq_ref, wqr_ref, wk_ref, wv_ref,
                     q_ref, k_ref, v_ref):
    cq = mc_ref[:, 0:MLA_Q_RANK]
    ckv = mc_ref[:, MLA_Q_RANK:MLA_Q_RANK + MLA_KV_RANK]
    cqn = (cq * lax.rsqrt(jnp.mean(cq * cq, axis=-1, keepdims=True) + NORM_EPS) * qg_ref[...]).astype(bf16)
    ckvn = (ckv * lax.rsqrt(jnp.mean(ckv * ckv, axis=-1, keepdims=True) + NORM_EPS) * kg_ref[...]).astype(bf16)
    cos = cos_ref[...]
    sin = sin_ref[...]
    k_rope = kr_ref[:, 0:LANES] * cos + kr_ref[:, LANES:2 * LANES] * sin
    v_ref[...] = jnp.dot(ckvn, wv_ref[...], preferred_element_type=f32).astype(bf16)
    for h in range(MLA_HEADS):
        sl = slice(h * LANES, (h + 1) * LANES)
        qh = jnp.dot(cqn, wq_ref[:, sl], preferred_element_type=f32)
        qr = jnp.dot(cqn, wqr_ref[:, sl], preferred_element_type=f32)
        q_ref[:, sl] = (qh * cos + qr * sin).astype(bf16)
        kh = jnp.dot(ckvn, wk_ref[:, sl], preferred_element_type=f32)
        k_ref[:, sl] = (kh + k_rope).astype(bf16)


def mla_prep(mc, kr, cos_t, sin_t, q_gain, kv_gain, wq, wqr, wk, wv, tm=512):
    t = mc.shape[0]
    tm = min(tm, t)
    n = MLA_HEADS * LANES
    tok = lambda w: pl.BlockSpec((tm, w), lambda i: (i, 0))
    full = lambda r, c: pl.BlockSpec((r, c), lambda i: (0, 0))
    return pl.pallas_call(
        _mla_prep_kernel,
        grid=(t // tm,),
        in_specs=[tok(SEG_MC), tok(SEG_KR), tok(LANES), tok(LANES),
                  full(1, MLA_Q_RANK), full(1, MLA_KV_RANK),
                  full(MLA_Q_RANK, n), full(MLA_Q_RANK, n), full(MLA_KV_RANK, n), full(MLA_KV_RANK, n)],
        out_specs=[tok(n)] * 3,
        out_shape=[jax.ShapeDtypeStruct((t, n), bf16)] * 3,
        compiler_params=_cparams("arbitrary"),
        name="mla_prep",
    )(mc, kr, cos_t, sin_t, q_gain.reshape(1, -1), kv_gain.reshape(1, -1), wq, wqr, wk, wv)


def _mla_attn_kernel(q_ref, k_ref, v_ref, o_ref):
    n = pl.program_id(1)
    qb = Q_BLOCK
    scale = (MLA_NOPE + MLA_ROPE) ** -0.5
    q_idx = n * qb + lax.broadcasted_iota(jnp.int32, (qb, qb), 0)
    lane = lax.broadcasted_iota(jnp.int32, (qb, LANES), 1)
    low = lane < HEAD
    npair = MLA_HEADS // 2

    def chunk(j, carry):
        ms, ls, accs = carry
        r0 = pl.multiple_of(j * qb, qb)
        k_idx = j * qb + lax.broadcasted_iota(jnp.int32, (qb, qb), 1)
        causal = k_idx <= q_idx
        new_ms, new_ls, new_accs = [], [], []
        for pr in range(npair):
            acc = accs[pr]
            alphas, pvs = [], []
            for hh in range(2):
                h = 2 * pr + hh
                sl = slice(h * LANES, (h + 1) * LANES)
                s = lax.dot_general(q_ref[:, sl], k_ref[pl.ds(r0, qb), sl],
                                    (((1,), (1,)), ((), ())), preferred_element_type=f32) * scale
                s = jnp.where(causal, s, -jnp.inf)
                m_new = jnp.maximum(ms[h], jnp.max(s, axis=-1, keepdims=True))
                alpha = jnp.exp(ms[h] - m_new)
                e = jnp.exp(s - m_new)
                new_ls.append(alpha * ls[h] + jnp.sum(e, axis=-1, keepdims=True))
                new_ms.append(m_new)
                alphas.append(alpha)
                pvs.append(jnp.dot(e.astype(bf16), v_ref[pl.ds(r0, qb), sl], preferred_element_type=f32))
            acc = acc * jnp.where(low, alphas[0], alphas[1]) + pvs[0] + pvs[1]
            new_accs.append(acc)
        return tuple(new_ms), tuple(new_ls), tuple(new_accs)

    init = (tuple(jnp.full((qb, 1), -jnp.inf, f32) for _ in range(MLA_HEADS)),
            tuple(jnp.zeros((qb, 1), f32) for _ in range(MLA_HEADS)),
            tuple(jnp.zeros((qb, LANES), f32) for _ in range(npair)))
    ms, ls, accs = lax.fori_loop(0, n + 1, chunk, init)
    for pr in range(npair):
        den = jnp.where(low, ls[2 * pr], ls[2 * pr + 1])
        o_ref[:, pr * LANES:(pr + 1) * LANES] = accs[pr] / den


def mla_attn(q, k, v, batch, seq):
    t = q.shape[0]
    nq = seq // Q_BLOCK
    n = MLA_HEADS * LANES
    return pl.pallas_call(
        _mla_attn_kernel,
        grid=(batch, nq),
        in_specs=[pl.BlockSpec((Q_BLOCK, n), lambda b, i: (b * nq + i, 0)),
                  pl.BlockSpec((seq, n), lambda b, i: (b, 0)),
                  pl.BlockSpec((seq, n), lambda b, i: (b, 0))],
        out_specs=pl.BlockSpec((Q_BLOCK, GROUP), lambda b, i: (b * nq + i, 0)),
        out_shape=jax.ShapeDtypeStruct((t, GROUP), f32),
        compiler_params=_cparams("arbitrary", "arbitrary"),
        name="mla_attn",
    )(q, k, v)


def _outproj_kernel(ysc_ref, ymla_ref, ycf_ref, ysw_ref, x_ref, mg_ref, w_ref, g_ref, b_ref, h_ref):
    tm = x_ref.shape[0]
    low = lax.broadcasted_iota(jnp.int32, (tm, LANES), 1) < HEAD
    acc = ALPHA * x_ref[...]
    for j, y_ref in enumerate((ysc_ref, ymla_ref, ycf_ref, ysw_ref)):
        for c in range(GROUP // LANES):
            col = j * GROUP + c * LANES
            y = y_ref[:, c * LANES:(c + 1) * LANES]
            y2 = y * y
            s_lo = jnp.sum(jnp.where(low, y2, 0.0), axis=-1, keepdims=True)
            s_hi = jnp.sum(jnp.where(low, 0.0, y2), axis=-1, keepdims=True)
            ms = jnp.where(low, s_lo, s_hi) * (1.0 / HEAD)
            yn = (y * lax.rsqrt(ms + NORM_EPS) * mg_ref[:, col:col + LANES]).astype(bf16)
            acc = acc + jnp.dot(yn, w_ref[col:col + LANES, :], preferred_element_type=f32)
    h_ref[...] = _layer_norm(acc, g_ref[...], b_ref[...])


def outproj(ysc, ymla, ycf, ysw, x2d, mix_g, w_out, ln_g, ln_b, tm=512):
    t = x2d.shape[0]
    tm = min(tm, t)
    tok = lambda w: pl.BlockSpec((tm, w), lambda i: (i, 0))
    full = lambda r, c: pl.BlockSpec((r, c), lambda i: (0, 0))
    return pl.pallas_call(
        _outproj_kernel,
        grid=(t // tm,),
        in_specs=[tok(GROUP)] * 4 + [tok(D_MODEL), full(1, D_MODEL), full(D_MODEL, D_MODEL),
                                     full(1, D_MODEL), full(1, D_MODEL)],
        out_specs=tok(D_MODEL),
        out_shape=jax.ShapeDtypeStruct((t, D_MODEL), f32),
        compiler_params=_cparams("arbitrary"),
        name="outproj",
    )(ysc, ymla, ycf, ysw, x2d, mix_g.reshape(1, -1), w_out, ln_g.reshape(1, -1), ln_b.reshape(1, -1))


def _topk_rows(s, payload, k):
    rows = s.shape[0]
    rid = lax.broadcasted_iota(jnp.int32, s.shape, 0)
    vals, pays = [], []
    for _ in range(k):
        m = jnp.max(s, axis=0, keepdims=True)
        first = jnp.min(jnp.where(s == m, rid, rows), axis=0, keepdims=True)
        hit = rid == first
        vals.append(m)
        pays.append(jnp.sum(jnp.where(hit, payload, 0), axis=0, keepdims=True))
        s = jnp.where(hit, -jnp.inf, s)
    return jnp.concatenate(vals, axis=0), jnp.concatenate(pays, axis=0)


def _peer_route_kernel(h_ref, wq_ref, keys_ref, idx_ref, gate_ref):
    tm = h_ref.shape[0]
    k = PEER_TOPK
    q = jnp.dot(h_ref[...].astype(bf16), wq_ref[...], preferred_element_type=f32).astype(bf16)
    key_id = lax.broadcasted_iota(jnp.int32, (PEER_NKEYS, tm), 0)
    for h in range(PEER_HEADS):
        tops = []
        for p in range(2):
            hp = 2 * h + p
            st = lax.dot_general(keys_ref[hp], q[:, hp * PEER_HALF:(hp + 1) * PEER_HALF],
                                 (((1,), (1,)), ((), ())), preferred_element_type=f32)
            tops.append(_topk_rows(st, key_id, k))
        (a, ia), (b, ib) = tops
        rep = lambda v: jnp.concatenate([jnp.broadcast_to(v[i:i + 1, :], (k, tm)) for i in range(k)], axis=0)
        til = lambda v: jnp.concatenate([v] * k, axis=0)
        cand_s = rep(a) + til(b)
        cand_i = rep(ia) * PEER_NKEYS + til(ib)
        best_s, best_i = _topk_rows(cand_s, cand_i, k)
        e = jnp.exp(best_s - jnp.max(best_s, axis=0, keepdims=True))
        gate_ref[h * k:(h + 1) * k, :] = e / jnp.sum(e, axis=0, keepdims=True)
        idx_ref[h * k:(h + 1) * k, :] = best_i


def peer_route(h2d, wq, keys, tm=256):
    t = h2d.shape[0]
    tm = min(tm, t)
    nq = wq.shape[1]
    return pl.pallas_call(
        _peer_route_kernel,
        grid=(t // tm,),
        in_specs=[pl.BlockSpec((tm, D_MODEL), lambda i: (i, 0)),
                  pl.BlockSpec((D_MODEL, nq), lambda i: (0, 0)),
                  pl.BlockSpec(keys.shape, lambda i: (0, 0, 0))],
        out_specs=[pl.BlockSpec((PEER_SLOTS, tm), lambda i: (0, i))] * 2,
        out_shape=[jax.ShapeDtypeStruct((PEER_SLOTS, t), jnp.int32), jax.ShapeDtypeStruct((PEER_SLOTS, t), f32)],
        compiler_params=_cparams("arbitrary"),
        name="peer_route",
    )(h2d, wq, keys)


PEER_TILE = 128
PEER_NBUF = 4


def _peer_apply_kernel(idx_hbm, gate_ref, h_ref, tab_hbm, g_ref, b_ref, o_ref,
                       idx_smem, gbuf, ybuf, isem, gsem):
    tt = h_ref.shape[0]
    d = D_MODEL
    cp = pltpu.make_async_copy(idx_hbm.at[pl.program_id(0)], idx_smem, isem)
    cp.start()
    cp.wait()

    def issue(t, buf):
        for s in range(PEER_SLOTS):
            pltpu.make_async_copy(tab_hbm.at[idx_smem[t, s]], gbuf.at[buf, s], gsem.at[buf]).start()

    def wait(buf):
        pltpu.make_async_copy(tab_hbm.at[pl.ds(0, PEER_SLOTS)], gbuf.at[buf], gsem.at[buf]).wait()

    for t0 in range(PEER_NBUF - 1):
        issue(t0, t0)

    lane = lax.broadcasted_iota(jnp.int32, (PEER_SLOTS, tt), 1)

    def token(t, carry):
        buf = t % PEER_NBUF
        nxt = t + PEER_NBUF - 1

        @pl.when(nxt < tt)
        def _():
            issue(nxt, nxt % PEER_NBUF)

        wait(buf)
        xt = h_ref[pl.ds(t, 1), :]
        s = jnp.sum(gbuf[buf, :, 0:d] * xt, axis=-1, keepdims=True)
        act = 0.5 * s * (1.0 + lax.erf(s * (2.0 ** -0.5)))
        gate = jnp.sum(jnp.where(lane == t, gate_ref[...], 0.0), axis=-1, keepdims=True)
        y = jnp.sum(gbuf[buf, :, d:2 * d] * (gate * act), axis=0, keepdims=True)
        ybuf[pl.ds(t, 1), :] = y
        return carry

    lax.fori_loop(0, tt, token, 0)
    o_ref[...] = _layer_norm(ALPHA * h_ref[...] + ybuf[...], g_ref[...], b_ref[...])


def peer_apply(idx_t, gate_t, h2d, table, ln_g, ln_b):
    t = h2d.shape[0]
    tt = min(PEER_TILE, t)
    idx = idx_t.T.reshape(t // tt, tt, PEER_SLOTS)
    row = pl.BlockSpec((1, D_MODEL), lambda i: (0, 0))
    return pl.pallas_call(
        _peer_apply_kernel,
        grid=(t // tt,),
        in_specs=[pl.BlockSpec(memory_space=pl.ANY),
                  pl.BlockSpec((PEER_SLOTS, tt), lambda i: (0, i)),
                  pl.BlockSpec((tt, D_MODEL), lambda i: (i, 0)),
                  pl.BlockSpec(memory_space=pl.ANY), row, row],
        out_specs=pl.BlockSpec((tt, D_MODEL), lambda i: (i, 0)),
        out_shape=jax.ShapeDtypeStruct((t, D_MODEL), f32),
        scratch_shapes=[pltpu.SMEM((tt, PEER_SLOTS), jnp.int32),
                        pltpu.VMEM((PEER_NBUF, PEER_SLOTS, 2 * D_MODEL), f32),
                        pltpu.VMEM((tt, D_MODEL), f32),
                        pltpu.SemaphoreType.DMA,
                        pltpu.SemaphoreType.DMA((PEER_NBUF,))],
        compiler_params=_cparams("arbitrary"),
        name="peer_apply",
    )(idx, gate_t, h2d, table, ln_g.reshape(1, -1), ln_b.reshape(1, -1))


def kernel(x, positions, w_in, sc_conv_w, mla_q_norm, mla_kv_norm, mla_w_uq, mla_w_uk, mla_w_uv,
           cf_dw_w, cf_dw_b, cf_ln_g, cf_ln_b, swa_sinks, mix_norm_g, w_out, ln1_g, ln1_b,
           peer_w_q, peer_sub_keys, peer_u, peer_v, ln2_g, ln2_b):
    batch, seq, d = x.shape
    t = batch * seq
    depth = w_in.shape[0]
    in_map = _inproj_column_map()
    q_map, qr_map, k_map, v_map = _mla_weight_maps()
    cos_t, sin_t = rope_tables(positions)
    xc = x.reshape(t, d)
    for l in range(depth):
        w_cat = _relayout_cols(w_in[l], in_map).astype(bf16)
        sc, cf, mc, kr, swq, swk, swv = inproj(xc, w_cat)
        y_sc, y_cf = local_mix(sc, cf, sc_conv_w[l], cf_dw_w[l], cf_dw_b[l], cf_ln_g[l], cf_ln_b[l], batch, seq)
        y_sw = swa(swq, swk, swv, swa_sinks[l], batch, seq)
        mq, mk, mv = mla_prep(mc, kr, cos_t, sin_t, mla_q_norm[l], mla_kv_norm[l],
                              _relayout_cols(mla_w_uq[l], q_map).astype(bf16),
                              _relayout_cols(mla_w_uq[l], qr_map).astype(bf16),
                              _relayout_cols(mla_w_uk[l], k_map).astype(bf16),
                              _relayout_cols(mla_w_uv[l], v_map).astype(bf16))
        y_mla = mla_attn(mq, mk, mv, batch, seq)
        h = outproj(y_sc, y_mla, y_cf, y_sw, xc, mix_norm_g[l], w_out[l].astype(bf16), ln1_g[l], ln1_b[l])
        keys = peer_sub_keys[l].reshape(2 * PEER_HEADS, PEER_NKEYS, PEER_HALF).astype(bf16)
        idx_t, gate_t = peer_route(h, peer_w_q[l].astype(bf16), keys)
        table = jnp.concatenate([peer_u[l], peer_v[l]], axis=1)
        xc = peer_apply(idx_t, gate_t, h, table, ln2_g[l], ln2_b[l])
    return xc.reshape(batch, seq, d)
```

```python
import functools
import math

import numpy as np
import jax
import jax.numpy as jnp
from jax import lax
from jax.experimental import pallas as pl
from jax.experimental.pallas import tpu as pltpu

f32 = jnp.float32
bf16 = jnp.bfloat16

D_MODEL = 1024
DEPTH = 4
GROUP = 256
HEAD = 64
SC_KERNEL = 3
MLA_HEADS = 4
MLA_NOPE = 64
MLA_ROPE = 32
MLA_Q_RANK = 256
MLA_KV_RANK = 128
ROPE_THETA = 10000.0
Q_BLOCK = 128
CF_KERNEL = 31
SWA_Q_HEADS = 4
SWA_KV_HEADS = 2
WINDOW = 128
PEER_HEADS = 8
PEER_NKEYS = 128
PEER_TOPK = 16
PEER_HALF = 128
PEER_SLOTS = PEER_HEADS * PEER_TOPK
ALPHA = (2 * DEPTH) ** 0.25
NORM_EPS = 1e-5

LANES = 128
SUBLANES = 8
VMEM_LIMIT = 48 * 1024 * 1024

SEG_SC = 3 * GROUP
SEG_CF = 2 * GROUP
SEG_MC = MLA_Q_RANK + MLA_KV_RANK
SEG_KR = 2 * LANES
SEG_SWQ = SWA_Q_HEADS * LANES
SEG_SWK = SWA_KV_HEADS * LANES
SEG_SWV = 2 * SWA_KV_HEADS * LANES
SEGS = (SEG_SC, SEG_CF, SEG_MC, SEG_KR, SEG_SWQ, SEG_SWK, SEG_SWV)
SEG_OFF = tuple(int(v) for v in np.cumsum((0,) + SEGS))
IN_COLS_PADDED = SEG_OFF[-1]
ROPE_LANE0 = MLA_NOPE


def _inproj_column_map():
    o_sc, o_cq, o_ckv, o_kr, o_cf = 0, 768, 1024, 1152, 1184
    o_swq, o_swk, o_swv = 1696, 1952, 2080
    src = np.zeros(IN_COLS_PADDED, np.int32)
    sgn = np.zeros(IN_COLS_PADDED, np.float32)

    def put(dst, cols, sign=1.0):
        src[dst:dst + len(cols)] = cols
        sgn[dst:dst + len(cols)] = sign

    half = MLA_ROPE // 2
    put(SEG_OFF[0], o_sc + np.arange(SEG_SC))
    put(SEG_OFF[1], o_cf + np.arange(SEG_CF))
    put(SEG_OFF[2], o_cq + np.arange(SEG_MC))
    x1 = o_kr + np.arange(half)
    x2 = o_kr + half + np.arange(half)
    put(SEG_OFF[3] + ROPE_LANE0, x1)
    put(SEG_OFF[3] + ROPE_LANE0 + half, x2)
    put(SEG_OFF[3] + LANES + ROPE_LANE0, x2, -1.0)
    put(SEG_OFF[3] + LANES + ROPE_LANE0 + half, x1)
    for h in range(SWA_Q_HEADS):
        put(SEG_OFF[4] + h * LANES, o_swq + h * HEAD + np.arange(HEAD))
    for g in range(SWA_KV_HEADS):
        put(SEG_OFF[5] + g * LANES, o_swk + g * HEAD + np.arange(HEAD))
        put(SEG_OFF[6] + (2 * g) * LANES, o_swv + g * HEAD + np.arange(HEAD))
        put(SEG_OFF[6] + (2 * g + 1) * LANES + HEAD, o_swv + g * HEAD + np.arange(HEAD))
    return src, sgn


def _mla_weight_maps():
    half = MLA_ROPE // 2
    qd = MLA_NOPE + MLA_ROPE
    n = MLA_HEADS * LANES
    q_src = np.zeros(n, np.int32); q_sgn = np.zeros(n, np.float32)
    r_src = np.zeros(n, np.int32); r_sgn = np.zeros(n, np.float32)
    k_src = np.zeros(n, np.int32); k_sgn = np.zeros(n, np.float32)
    v_src = np.zeros(n, np.int32); v_sgn = np.zeros(n, np.float32)
    for h in range(MLA_HEADS):
        b = h * LANES
        q_src[b:b + qd] = h * qd + np.arange(qd); q_sgn[b:b + qd] = 1.0
        x1 = h * qd + MLA_NOPE + np.arange(half)
        x2 = x1 + half
        r_src[b + ROPE_LANE0:b + ROPE_LANE0 + half] = x2; r_sgn[b + ROPE_LANE0:b + ROPE_LANE0 + half] = -1.0
        r_src[b + ROPE_LANE0 + half:b + ROPE_LANE0 + 2 * half] = x1; r_sgn[b + ROPE_LANE0 + half:b + ROPE_LANE0 + 2 * half] = 1.0
        k_src[b:b + MLA_NOPE] = h * MLA_NOPE + np.arange(MLA_NOPE); k_sgn[b:b + MLA_NOPE] = 1.0
        lo = b + (HEAD if h % 2 else 0)
        v_src[lo:lo + HEAD] = h * HEAD + np.arange(HEAD); v_sgn[lo:lo + HEAD] = 1.0
    return (q_src, q_sgn), (r_src, r_sgn), (k_src, k_sgn), (v_src, v_sgn)


def _relayout_cols(w, src_sgn):
    src, sgn = src_sgn
    return jnp.take(w, jnp.asarray(src), axis=1) * jnp.asarray(sgn)[None, :]


def _layer_norm(v, g, b):
    mu = jnp.mean(v, axis=-1, keepdims=True)
    c = v - mu
    var = jnp.mean(c * c, axis=-1, keepdims=True)
    return c * lax.rsqrt(var + NORM_EPS) * g + b


def _cparams(*sem):
    return pltpu.CompilerParams(dimension_semantics=sem, vmem_limit_bytes=VMEM_LIMIT)


def _rope_kernel(pos_ref, inv_ref, cos_ref, sin_ref):
    ang = pos_ref[...].astype(f32) * inv_ref[...]
    cos_ref[...] = jnp.cos(ang)
    sin_ref[...] = jnp.sin(ang)


def rope_tables(positions, tm=1024):
    t = positions.size
    half = MLA_ROPE // 2
    inv = ROPE_THETA ** (-jnp.arange(half, dtype=f32) / half)
    inv_row = jnp.zeros((1, LANES), f32)
    inv_row = inv_row.at[0, ROPE_LANE0:ROPE_LANE0 + half].set(inv)
    inv_row = inv_row.at[0, ROPE_LANE0 + half:ROPE_LANE0 + 2 * half].set(inv)
    tm = min(tm, t)
    return pl.pallas_call(
        _rope_kernel,
        grid=(t // tm,),
        in_specs=[pl.BlockSpec((tm, 1), lambda i: (i, 0)), pl.BlockSpec((1, LANES), lambda i: (0, 0))],
        out_specs=[pl.BlockSpec((tm, LANES), lambda i: (i, 0))] * 2,
        out_shape=[jax.ShapeDtypeStruct((t, LANES), f32)] * 2,
        compiler_params=_cparams("arbitrary"),
        name="rope_tables",
    )(positions.reshape(t, 1), inv_row)


def _inproj_kernel(x_ref, w_ref, sc_ref, cf_ref, mc_ref, kr_ref, q_ref, k_ref, v_ref):
    xb = x_ref[...].astype(bf16)
    outs = (sc_ref, cf_ref, mc_ref, kr_ref, q_ref, k_ref, v_ref)
    for j, o_ref in enumerate(outs):
        acc = jnp.dot(xb, w_ref[:, SEG_OFF[j]:SEG_OFF[j + 1]], preferred_element_type=f32)
        o_ref[...] = acc.astype(o_ref.dtype)


def inproj(x2d, w_cat, tm=512):
    t = x2d.shape[0]
    tm = min(tm, t)
    dts = (f32, f32, f32, f32, bf16, bf16, bf16)
    return pl.pallas_call(
        _inproj_kernel,
        grid=(t // tm,),
        in_specs=[pl.BlockSpec((tm, D_MODEL), lambda i: (i, 0)),
                  pl.BlockSpec((D_MODEL, IN_COLS_PADDED), lambda i: (0, 0))],
        out_specs=[pl.BlockSpec((tm, w), lambda i: (i, 0)) for w in SEGS],
        out_shape=[jax.ShapeDtypeStruct((t, w), d) for w, d in zip(SEGS, dts)],
        compiler_params=_cparams("arbitrary"),
        name="inproj",
    )(x2d, w_cat)


CONV_ROWS = 64
SC_HALO = 8
CF_HALO = 32


def _local_mix_kernel(sc_ref, sch_ref, cf_ref, cfh_ref, scw_ref, cfw_ref, cfb_ref, lng_ref, lnb_ref,
                      ysc_ref, ycf_ref, pad_sc, pad_cf):
    ts = sc_ref.shape[0]
    first = pl.program_id(1) == 0
    g = GROUP

    pad_sc[SC_HALO:SC_HALO + ts, :] = sc_ref[:, g:2 * g] * sc_ref[:, 2 * g:3 * g]
    halo = sch_ref[:, g:2 * g] * sch_ref[:, 2 * g:3 * g]
    pad_sc[0:SC_HALO, :] = jnp.where(first, 0.0, halo)
    for c in range(ts // CONV_ROWS):
        r0 = c * CONV_ROWS
        acc = jnp.zeros((CONV_ROWS, g), f32)
        for k in range(SC_KERNEL):
            o = SC_HALO + r0 - (SC_KERNEL - 1) + k
            acc = acc + scw_ref[k:k + 1, :] * pad_sc[o:o + CONV_ROWS, :]
        ysc_ref[r0:r0 + CONV_ROWS, :] = sc_ref[r0:r0 + CONV_ROWS, 0:g] * acc

    pad_cf[CF_HALO:CF_HALO + ts, :] = cf_ref[:, 0:g] * jax.nn.sigmoid(cf_ref[:, g:2 * g])
    halo = cfh_ref[:, 0:g] * jax.nn.sigmoid(cfh_ref[:, g:2 * g])
    pad_cf[0:CF_HALO, :] = jnp.where(first, 0.0, halo)
    for c in range(ts // CONV_ROWS):
        r0 = c * CONV_ROWS
        acc = jnp.zeros((CONV_ROWS, g), f32) + cfb_ref[...]
        for k in range(CF_KERNEL):
            o = CF_HALO + r0 - (CF_KERNEL - 1) + k
            acc = acc + cfw_ref[k:k + 1, :] * pad_cf[o:o + CONV_ROWS, :]
        u = _layer_norm(acc, lng_ref[...], lnb_ref[...])
        ycf_ref[r0:r0 + CONV_ROWS, :] = u * jax.nn.sigmoid(u)


def local_mix(sc, cf, sc_w, cf_w, cf_b, ln_g, ln_b, batch, seq, ts=512):
    t = sc.shape[0]
    ts = min(ts, seq)
    nsb = seq // ts
    main = lambda b, n: (b * nsb + n, 0)

    def halo(rows):
        per = ts // rows
        return lambda b, n: (jnp.maximum((b * nsb + n) * per - 1, 0), 0)

    row = lambda w: pl.BlockSpec((1, w), lambda b, n: (0, 0))
    return pl.pallas_call(
        _local_mix_kernel,
        grid=(batch, nsb),
        in_specs=[pl.BlockSpec((ts, SEG_SC), main), pl.BlockSpec((SC_HALO, SEG_SC), halo(SC_HALO)),
                  pl.BlockSpec((ts, SEG_CF), main), pl.BlockSpec((CF_HALO, SEG_CF), halo(CF_HALO)),
                  pl.BlockSpec((SC_KERNEL, GROUP), lambda b, n: (0, 0)),
                  pl.BlockSpec((CF_KERNEL, GROUP), lambda b, n: (0, 0)),
                  row(GROUP), row(GROUP), row(GROUP)],
        out_specs=[pl.BlockSpec((ts, GROUP), main)] * 2,
        out_shape=[jax.ShapeDtypeStruct((t, GROUP), f32)] * 2,
        scratch_shapes=[pltpu.VMEM((SC_HALO + ts, GROUP), f32), pltpu.VMEM((CF_HALO + ts, GROUP), f32)],
        compiler_params=_cparams("arbitrary", "arbitrary"),
        name="local_mix",
    )(sc, sc, cf, cf, sc_w, cf_w, cf_b.reshape(1, GROUP), ln_g.reshape(1, GROUP), ln_b.reshape(1, GROUP))


def _swa_kernel(sinks_ref, q_ref, kc_ref, kp_ref, vc_ref, vp_ref, o_ref):
    n = pl.program_id(1)
    w = WINDOW
    kk = jnp.concatenate([kp_ref[...], kc_ref[...]], axis=0)
    vv = jnp.concatenate([vp_ref[...], vc_ref[...]], axis=0)
    qi = lax.broadcasted_iota(jnp.int32, (w, 2 * w), 0)
    kj = lax.broadcasted_iota(jnp.int32, (w, 2 * w), 1)
    dist = qi + w - kj
    valid = (dist >= 0) & (dist < w) & ((n > 0) | (kj >= w))
    distf = dist.astype(f32)
    group = SWA_Q_HEADS // SWA_KV_HEADS
    for g in range(SWA_KV_HEADS):
        acc = jnp.zeros((w, LANES), f32)
        for gi in range(group):
            hq = g * group + gi
            slope = 2.0 ** (-8.0 * (hq + 1) / SWA_Q_HEADS)
            s = lax.dot_general(q_ref[:, hq * LANES:(hq + 1) * LANES], kk[:, g * LANES:(g + 1) * LANES],
                                (((1,), (1,)), ((), ())), preferred_element_type=f32)
            s = s * (HEAD ** -0.5) - slope * distf
            s = jnp.where(valid, s, -jnp.inf)
            sink = sinks_ref[hq]
            m = jnp.maximum(jnp.max(s, axis=-1, keepdims=True), sink)
            e = jnp.exp(s - m)
            den = jnp.sum(e, axis=-1, keepdims=True) + jnp.exp(sink - m)
            p = (e / den).astype(bf16)
            blk = 2 * g + gi
            acc = acc + jnp.dot(p, vv[:, blk * LANES:(blk + 1) * LANES], preferred_element_type=f32)
        o_ref[:, g * LANES:(g + 1) * LANES] = acc


def swa(q, k, v, sinks, batch, seq):
    t = q.shape[0]
    nb = seq // WINDOW
    cur = lambda b, n: (b * nb + n, 0)
    prev = lambda b, n: (b * nb + jnp.maximum(n - 1, 0), 0)
    return pl.pallas_call(
        _swa_kernel,
        grid=(batch, nb),
        in_specs=[pl.BlockSpec(memory_space=pltpu.SMEM),
                  pl.BlockSpec((WINDOW, SEG_SWQ), cur),
                  pl.BlockSpec((WINDOW, SEG_SWK), cur), pl.BlockSpec((WINDOW, SEG_SWK), prev),
                  pl.BlockSpec((WINDOW, SEG_SWV), cur), pl.BlockSpec((WINDOW, SEG_SWV), prev)],
        out_specs=pl.BlockSpec((WINDOW, GROUP), cur),
        out_shape=jax.ShapeDtypeStruct((t, GROUP), f32),
        compiler_params=_cparams("arbitrary", "arbitrary"),
        name="swa",
    )(sinks, q, k, k, v, v)


def _mla_prep_kernel(mc_ref, kr_ref, cos_ref, sin_ref, qg_ref, kg_ref, wq_ref, wqr_ref, wk_ref, wv_ref,
                     q_ref, k_ref, v_ref):
    cq = mc_ref[:, 0:MLA_Q_RANK]
    ckv = mc_ref[:, MLA_Q_RANK:MLA_Q_RANK + MLA_KV_RANK]
    cqn = (cq * lax.rsqrt(jnp.mean(cq * cq, axis=-1, keepdims=True) + NORM_EPS) * qg_ref[...]).astype(bf16)
    ckvn = (ckv * lax.rsqrt(jnp.mean(ckv * ckv, axis=-1, keepdims=True) + NORM_EPS) * kg_ref[...]).astype(bf16)
    cos = cos_ref[...]
    sin = sin_ref[...]
    k_rope = kr_ref[:, 0:LANES] * cos + kr_ref[:, LANES:2 * LANES] * sin
    v_ref[...] = jnp.dot(ckvn, wv_ref[...], preferred_element_type=f32).astype(bf16)
    for h in range(MLA_HEADS):
        sl = slice(h * LANES, (h + 1) * LANES)
        qh = jnp.dot(cqn, wq_ref[:, sl], preferred_element_type=f32)
        qr = jnp.dot(cqn, wqr_ref[:, sl], preferred_element_type=f32)
        q_ref[:, sl] = (qh * cos + qr * sin).astype(bf16)
        kh = jnp.dot(ckvn, wk_ref[:, sl], preferred_element_type=f32)
        k_ref[:, sl] = (kh + k_rope).astype(bf16)


def mla_prep(mc, kr, cos_t, sin_t, q_gain, kv_gain, wq, wqr, wk, wv, tm=512):
    t = mc.shape[0]
    tm = min(tm, t)
    n = MLA_HEADS * LANES
    tok = lambda w: pl.BlockSpec((tm, w), lambda i: (i, 0))
    full = lambda r, c: pl.BlockSpec((r, c), lambda i: (0, 0))
    return pl.pallas_call(
        _mla_prep_kernel,
        grid=(t // tm,),
        in_specs=[tok(SEG_MC), tok(SEG_KR), tok(LANES), tok(LANES),
                  full(1, MLA_Q_RANK), full(1, MLA_KV_RANK),
                  full(MLA_Q_RANK, n), full(MLA_Q_RANK, n), full(MLA_KV_RANK, n), full(MLA_KV_RANK, n)],
        out_specs=[tok(n)] * 3,
        out_shape=[jax.ShapeDtypeStruct((t, n), bf16)] * 3,
        compiler_params=_cparams("arbitrary"),
        name="mla_prep",
    )(mc, kr, cos_t, sin_t, q_gain.reshape(1, -1), kv_gain.reshape(1, -1), wq, wqr, wk, wv)


def _mla_attn_kernel(q_ref, k_ref, v_ref, o_ref):
    n = pl.program_id(1)
    qb = Q_BLOCK
    scale = (MLA_NOPE + MLA_ROPE) ** -0.5
    q_idx = n * qb + lax.broadcasted_iota(jnp.int32, (qb, qb), 0)
    lane = lax.broadcasted_iota(jnp.int32, (qb, LANES), 1)
    low = lane < HEAD
    npair = MLA_HEADS // 2

    def chunk(j, carry):
        ms, ls, accs = carry
        r0 = pl.multiple_of(j * qb, qb)
        k_idx = j * qb + lax.broadcasted_iota(jnp.int32, (qb, qb), 1)
        causal = k_idx <= q_idx
        new_ms, new_ls, new_accs = [], [], []
        for pr in range(npair):
            acc = accs[pr]
            alphas, pvs = [], []
            for hh in range(2):
                h = 2 * pr + hh
                sl = slice(h * LANES, (h + 1) * LANES)
                s = lax.dot_general(q_ref[:, sl], k_ref[pl.ds(r0, qb), sl],
                                    (((1,), (1,)), ((), ())), preferred_element_type=f32) * scale
                s = jnp.where(causal, s, -jnp.inf)
                m_new = jnp.maximum(ms[h], jnp.max(s, axis=-1, keepdims=True))
                alpha = jnp.exp(ms[h] - m_new)
                e = jnp.exp(s - m_new)
                new_ls.append(alpha * ls[h] + jnp.sum(e, axis=-1, keepdims=True))
                new_ms.append(m_new)
                alphas.append(alpha)
                pvs.append(jnp.dot(e.astype(bf16), v_ref[pl.ds(r0, qb), sl], preferred_element_type=f32))
            acc = acc * jnp.where(low, alphas[0], alphas[1]) + pvs[0] + pvs[1]
            new_accs.append(acc)
        return tuple(new_ms), tuple(new_ls), tuple(new_accs)

    init = (tuple(jnp.full((qb, 1), -jnp.inf, f32) for _ in range(MLA_HEADS)),
            tuple(jnp.zeros((qb, 1), f32) for _ in range(MLA_HEADS)),
            tuple(jnp.zeros((qb, LANES), f32) for _ in range(npair)))
    ms, ls, accs = lax.fori_loop(0, n + 1, chunk, init)
    for pr in range(npair):
        den = jnp.where(low, ls[2 * pr], ls[2 * pr + 1])
        o_ref[:, pr * LANES:(pr + 1) * LANES] = accs[pr] / den


def mla_attn(q, k, v, batch, seq):
    t = q.shape[0]
    nq = seq // Q_BLOCK
    n = MLA_HEADS * LANES
    return pl.pallas_call(
        _mla_attn_kernel,
        grid=(batch, nq),
        in_specs=[pl.BlockSpec((Q_BLOCK, n), lambda b, i: (b * nq + i, 0)),
                  pl.BlockSpec((seq, n), lambda b, i: (b, 0)),
                  pl.BlockSpec((seq, n), lambda b, i: (b, 0))],
        out_specs=pl.BlockSpec((Q_BLOCK, GROUP), lambda b, i: (b * nq + i, 0)),
        out_shape=jax.ShapeDtypeStruct((t, GROUP), f32),
        compiler_params=_cparams("arbitrary", "arbitrary"),
        name="mla_attn",
    )(q, k, v)


def _outproj_kernel(ysc_ref, ymla_ref, ycf_ref, ysw_ref, x_ref, mg_ref, w_ref, g_ref, b_ref, h_ref):
    tm = x_ref.shape[0]
    low = lax.broadcasted_iota(jnp.int32, (tm, LANES), 1) < HEAD
    acc = ALPHA * x_ref[...]
    for j, y_ref in enumerate((ysc_ref, ymla_ref, ycf_ref, ysw_ref)):
        for c in range(GROUP // LANES):
            col = j * GROUP + c * LANES
            y = y_ref[:, c * LANES:(c + 1) * LANES]
            y2 = y * y
            s_lo = jnp.sum(jnp.where(low, y2, 0.0), axis=-1, keepdims=True)
            s_hi = jnp.sum(jnp.where(low, 0.0, y2), axis=-1, keepdims=True)
            ms = jnp.where(low, s_lo, s_hi) * (1.0 / HEAD)
            yn = (y * lax.rsqrt(ms + NORM_EPS) * mg_ref[:, col:col + LANES]).astype(bf16)
            acc = acc + jnp.dot(yn, w_ref[col:col + LANES, :], preferred_element_type=f32)
    h_ref[...] = _layer_norm(acc, g_ref[...], b_ref[...])


def outproj(ysc, ymla, ycf, ysw, x2d, mix_g, w_out, ln_g, ln_b, tm=512):
    t = x2d.shape[0]
    tm = min(tm, t)
    tok = lambda w: pl.BlockSpec((tm, w), lambda i: (i, 0))
    full = lambda r, c: pl.BlockSpec((r, c), lambda i: (0, 0))
    return pl.pallas_call(
        _outproj_kernel,
        grid=(t // tm,),
        in_specs=[tok(GROUP)] * 4 + [tok(D_MODEL), full(1, D_MODEL), full(D_MODEL, D_MODEL),
                                     full(1, D_MODEL), full(1, D_MODEL)],
        out_specs=tok(D_MODEL),
        out_shape=jax.ShapeDtypeStruct((t, D_MODEL), f32),
        compiler_params=_cparams("arbitrary"),
        name="outproj",
    )(ysc, ymla, ycf, ysw, x2d, mix_g.reshape(1, -1), w_out, ln_g.reshape(1, -1), ln_b.reshape(1, -1))


def _topk_rows(s, payload, k):
    rows = s.shape[0]
    rid = lax.broadcasted_iota(jnp.int32, s.shape, 0)
    vals, pays = [], []
    for _ in range(k):
        m = jnp.max(s, axis=0, keepdims=True)
        first = jnp.min(jnp.where(s == m, rid, rows), axis=0, keepdims=True)
        hit = rid == first
        vals.append(m)
        pays.append(jnp.sum(jnp.where(hit, payload, 0), axis=0, keepdims=True))
        s = jnp.where(hit, -jnp.inf, s)
    return jnp.concatenate(vals, axis=0), jnp.concatenate(pays, axis=0)


def _peer_route_kernel(h_ref, wq_ref, keys_ref, idx_ref, gate_ref):
    tm = h_ref.shape[0]
    k = PEER_TOPK
    q = jnp.dot(h_ref[...].astype(bf16), wq_ref[...], preferred_element_type=f32).astype(bf16)
    key_id = lax.broadcasted_iota(jnp.int32, (PEER_NKEYS, tm), 0)
    for h in range(PEER_HEADS):
        tops = []
        for p in range(2):
            hp = 2 * h + p
            st = lax.dot_general(keys_ref[hp], q[:, hp * PEER_HALF:(hp + 1) * PEER_HALF],
                                 (((1,), (1,)), ((), ())), preferred_element_type=f32)
            tops.append(_topk_rows(st, key_id, k))
        (a, ia), (b, ib) = tops
        rep = lambda v: jnp.concatenate([jnp.broadcast_to(v[i:i + 1, :], (k, tm)) for i in range(k)], axis=0)
        til = lambda v: jnp.concatenate([v] * k, axis=0)
        cand_s = rep(a) + til(b)
        cand_i = rep(ia) * PEER_NKEYS + til(ib)
        best_s, best_i = _topk_rows(cand_s, cand_i, k)
        e = jnp.exp(best_s - jnp.max(best_s, axis=0, keepdims=True))
        gate_ref[h * k:(h + 1) * k, :] = e / jnp.sum(e, axis=0, keepdims=True)
        idx_ref[h * k:(h + 1) * k, :] = best_i


def peer_route(h2d, wq, keys, tm=256):
    t = h2d.shape[0]
    tm = min(tm, t)
    nq = wq.shape[1]
    return pl.pallas_call(
        _peer_route_kernel,
        grid=(t // tm,),
        in_specs=[pl.BlockSpec((tm, D_MODEL), lambda i: (i, 0)),
                  pl.BlockSpec((D_MODEL, nq), lambda i: (0, 0)),
                  pl.BlockSpec(keys.shape, lambda i: (0, 0, 0))],
        out_specs=[pl.BlockSpec((PEER_SLOTS, tm), lambda i: (0, i))] * 2,
        out_shape=[jax.ShapeDtypeStruct((PEER_SLOTS, t), jnp.int32), jax.ShapeDtypeStruct((PEER_SLOTS, t), f32)],
        compiler_params=_cparams("arbitrary"),
        name="peer_route",
    )(h2d, wq, keys)


PEER_TILE = 128
PEER_NBUF = 8
D_CHUNKS = D_MODEL // LANES
ISSUE_GROUP = PEER_SLOTS // (2 * D_CHUNKS)


def _peer_apply_kernel(idx_hbm, gate_ref, h_ref, tab_hbm, g_ref, b_ref, o_ref,
                       idx_smem, gbuf, ybuf, isem, gsem):
    tt = h_ref.shape[0]
    nb = PEER_NBUF
    cp = pltpu.make_async_copy(idx_hbm.at[pl.program_id(0)], idx_smem, isem)
    cp.start()
    cp.wait()

    def issue(t, buf, slots):
        for s in slots:
            pltpu.make_async_copy(tab_hbm.at[idx_smem[t, s]], gbuf.at[buf, :, s, :],
                                  gsem.at[buf]).start(priority=s % 2)

    def wait(buf):
        pltpu.make_async_copy(gbuf.at[buf], gbuf.at[buf], gsem.at[buf]).wait()

    lane = lax.broadcasted_iota(jnp.int32, (PEER_SLOTS, tt), 1)

    def token(t, buf, prefetch):
        nxt, nbuf = t + nb - 1, (buf + nb - 1) % nb
        groups = [range(g * ISSUE_GROUP, (g + 1) * ISSUE_GROUP) for g in range(2 * D_CHUNKS)]
        wait(buf)
        xt = h_ref[pl.ds(t, 1), :]
        acc = None
        for c in range(D_CHUNKS):
            if prefetch:
                issue(nxt, nbuf, groups[c])
            term = gbuf[buf, c] * xt[:, c * LANES:(c + 1) * LANES]
            acc = term if acc is None else acc + term
        s = jnp.sum(acc, axis=-1, keepdims=True)
        act = 0.5 * s * (1.0 + lax.erf(s * (2.0 ** -0.5)))
        gate = jnp.sum(jnp.where(lane == t, gate_ref[...], 0.0), axis=-1, keepdims=True)
        w = gate * act
        ys = []
        for c in range(D_CHUNKS):
            if prefetch:
                issue(nxt, nbuf, groups[D_CHUNKS + c])
            ys.append(jnp.sum(gbuf[buf, D_CHUNKS + c] * w, axis=0, keepdims=True))
        ybuf[pl.ds(t, 1), :] = jnp.concatenate(ys, axis=-1)

    for t0 in range(nb - 1):
        issue(t0, t0, range(PEER_SLOTS))

    def group(gi, carry):
        for b in range(nb):
            token(gi * nb + b, b, True)
        return carry

    ngroups = tt // nb
    lax.fori_loop(0, ngroups - 1, group, 0)
    for b in range(nb):
        t = (ngroups - 1) * nb + b
        token(t, b, t + nb - 1 < tt)
    o_ref[...] = _layer_norm(ALPHA * h_ref[...] + ybuf[...], g_ref[...], b_ref[...])


def peer_apply(idx_t, gate_t, h2d, table, ln_g, ln_b):
    t = h2d.shape[0]
    tt = min(PEER_TILE, t)
    assert tt % PEER_NBUF == 0 and tt >= 2 * PEER_NBUF
    idx = idx_t.T.reshape(t // tt, tt, PEER_SLOTS)
    row = pl.BlockSpec((1, D_MODEL), lambda i: (0, 0))
    return pl.pallas_call(
        _peer_apply_kernel,
        grid=(t // tt,),
        in_specs=[pl.BlockSpec(memory_space=pl.ANY),
                  pl.BlockSpec((PEER_SLOTS, tt), lambda i: (0, i)),
                  pl.BlockSpec((tt, D_MODEL), lambda i: (i, 0)),
                  pl.BlockSpec(memory_space=pl.ANY), row, row],
        out_specs=pl.BlockSpec((tt, D_MODEL), lambda i: (i, 0)),
        out_shape=jax.ShapeDtypeStruct((t, D_MODEL), f32),
        scratch_shapes=[pltpu.SMEM((tt, PEER_SLOTS), jnp.int32),
                        pltpu.VMEM((PEER_NBUF, 2 * D_CHUNKS, PEER_SLOTS, LANES), f32),
                        pltpu.VMEM((tt, D_MODEL), f32),
                        pltpu.SemaphoreType.DMA,
                        pltpu.SemaphoreType.DMA((PEER_NBUF,))],
        compiler_params=_cparams("arbitrary"),
        name="peer_apply",
    )(idx, gate_t, h2d, table, ln_g.reshape(1, -1), ln_b.reshape(1, -1))


def kernel(x, positions, w_in, sc_conv_w, mla_q_norm, mla_kv_norm, mla_w_uq, mla_w_uk, mla_w_uv,
           cf_dw_w, cf_dw_b, cf_ln_g, cf_ln_b, swa_sinks, mix_norm_g, w_out, ln1_g, ln1_b,
           peer_w_q, peer_sub_keys, peer_u, peer_v, ln2_g, ln2_b):
    batch, seq, d = x.shape
    t = batch * seq
    depth = w_in.shape[0]
    in_map = _inproj_column_map()
    q_map, qr_map, k_map, v_map = _mla_weight_maps()
    cos_t, sin_t = rope_tables(positions)
    xc = x.reshape(t, d)
    for l in range(depth):
        w_cat = _relayout_cols(w_in[l], in_map).astype(bf16)
        sc, cf, mc, kr, swq, swk, swv = inproj(xc, w_cat)
        y_sc, y_cf = local_mix(sc, cf, sc_conv_w[l], cf_dw_w[l], cf_dw_b[l], cf_ln_g[l], cf_ln_b[l], batch, seq)
        y_sw = swa(swq, swk, swv, swa_sinks[l], batch, seq)
        mq, mk, mv = mla_prep(mc, kr, cos_t, sin_t, mla_q_norm[l], mla_kv_norm[l],
                              _relayout_cols(mla_w_uq[l], q_map).astype(bf16),
                              _relayout_cols(mla_w_uq[l], qr_map).astype(bf16),
                              _relayout_cols(mla_w_uk[l], k_map).astype(bf16),
                              _relayout_cols(mla_w_uv[l], v_map).astype(bf16))
        y_mla = mla_attn(mq, mk, mv, batch, seq)
        h = outproj(y_sc, y_mla, y_cf, y_sw, xc, mix_norm_g[l], w_out[l].astype(bf16), ln1_g[l], ln1_b[l])
        keys = peer_sub_keys[l].reshape(2 * PEER_HEADS, PEER_NKEYS, PEER_HALF).astype(bf16)
        idx_t, gate_t = peer_route(h, peer_w_q[l].astype(bf16), keys)
        experts = peer_u.shape[1]
        table = jnp.concatenate([peer_u[l].reshape(experts, D_CHUNKS, LANES),
                                 peer_v[l].reshape(experts, D_CHUNKS, LANES)], axis=1)
        xc = peer_apply(idx_t, gate_t, h, table, ln2_g[l], ln2_b[l])
    return xc.reshape(batch, seq, d)
```

```python
import functools
import math

import numpy as np
import jax
import jax.numpy as jnp
from jax import lax
from jax.experimental import pallas as pl
from jax.experimental.pallas import tpu as pltpu

f32 = jnp.float32
bf16 = jnp.bfloat16

D_MODEL = 1024
DEPTH = 4
GROUP = 256
HEAD = 64
SC_KERNEL = 3
MLA_HEADS = 4
MLA_NOPE = 64
MLA_ROPE = 32
MLA_Q_RANK = 256
MLA_KV_RANK = 128
ROPE_THETA = 10000.0
Q_BLOCK = 128
CF_KERNEL = 31
SWA_Q_HEADS = 4
SWA_KV_HEADS = 2
WINDOW = 128
PEER_HEADS = 8
PEER_NKEYS = 128
PEER_TOPK = 16
PEER_HALF = 128
PEER_SLOTS = PEER_HEADS * PEER_TOPK
ALPHA = (2 * DEPTH) ** 0.25
NORM_EPS = 1e-5

LANES = 128
SUBLANES = 8
VMEM_LIMIT = 48 * 1024 * 1024

SEG_SC = 3 * GROUP
SEG_CF = 2 * GROUP
SEG_MC = MLA_Q_RANK + MLA_KV_RANK
SEG_KR = 2 * LANES
SEG_SWQ = SWA_Q_HEADS * LANES
SEG_SWK = SWA_KV_HEADS * LANES
SEG_SWV = 2 * SWA_KV_HEADS * LANES
SEGS = (SEG_SC, SEG_CF, SEG_MC, SEG_KR, SEG_SWQ, SEG_SWK, SEG_SWV)
SEG_OFF = tuple(int(v) for v in np.cumsum((0,) + SEGS))
IN_COLS_PADDED = SEG_OFF[-1]
ROPE_LANE0 = MLA_NOPE


def _inproj_column_map():
    o_sc, o_cq, o_ckv, o_kr, o_cf = 0, 768, 1024, 1152, 1184
    o_swq, o_swk, o_swv = 1696, 1952, 2080
    src = np.zeros(IN_COLS_PADDED, np.int32)
    sgn = np.zeros(IN_COLS_PADDED, np.float32)

    def put(dst, cols, sign=1.0):
        src[dst:dst + len(cols)] = cols
        sgn[dst:dst + len(cols)] = sign

    half = MLA_ROPE // 2
    put(SEG_OFF[0], o_sc + np.arange(SEG_SC))
    put(SEG_OFF[1], o_cf + np.arange(SEG_CF))
    put(SEG_OFF[2], o_cq + np.arange(SEG_MC))
    x1 = o_kr + np.arange(half)
    x2 = o_kr + half + np.arange(half)
    put(SEG_OFF[3] + ROPE_LANE0, x1)
    put(SEG_OFF[3] + ROPE_LANE0 + half, x2)
    put(SEG_OFF[3] + LANES + ROPE_LANE0, x2, -1.0)
    put(SEG_OFF[3] + LANES + ROPE_LANE0 + half, x1)
    for h in range(SWA_Q_HEADS):
        put(SEG_OFF[4] + h * LANES, o_swq + h * HEAD + np.arange(HEAD))
    for g in range(SWA_KV_HEADS):
        put(SEG_OFF[5] + g * LANES, o_swk + g * HEAD + np.arange(HEAD))
        put(SEG_OFF[6] + (2 * g) * LANES, o_swv + g * HEAD + np.arange(HEAD))
        put(SEG_OFF[6] + (2 * g + 1) * LANES + HEAD, o_swv + g * HEAD + np.arange(HEAD))
    return src, sgn


def _mla_weight_maps():
    half = MLA_ROPE // 2
    qd = MLA_NOPE + MLA_ROPE
    n = MLA_HEADS * LANES
    q_src = np.zeros(n, np.int32); q_sgn = np.zeros(n, np.float32)
    r_src = np.zeros(n, np.int32); r_sgn = np.zeros(n, np.float32)
    k_src = np.zeros(n, np.int32); k_sgn = np.zeros(n, np.float32)
    v_src = np.zeros(n, np.int32); v_sgn = np.zeros(n, np.float32)
    for h in range(MLA_HEADS):
        b = h * LANES
        q_src[b:b + qd] = h * qd + np.arange(qd); q_sgn[b:b + qd] = 1.0
        x1 = h * qd + MLA_NOPE + np.arange(half)
        x2 = x1 + half
        r_src[b + ROPE_LANE0:b + ROPE_LANE0 + half] = x2; r_sgn[b + ROPE_LANE0:b + ROPE_LANE0 + half] = -1.0
        r_src[b + ROPE_LANE0 + half:b + ROPE_LANE0 + 2 * half] = x1; r_sgn[b + ROPE_LANE0 + half:b + ROPE_LANE0 + 2 * half] = 1.0
        k_src[b:b + MLA_NOPE] = h * MLA_NOPE + np.arange(MLA_NOPE); k_sgn[b:b + MLA_NOPE] = 1.0
        lo = b + (HEAD if h % 2 else 0)
        v_src[lo:lo + HEAD] = h * HEAD + np.arange(HEAD); v_sgn[lo:lo + HEAD] = 1.0
    return (q_src, q_sgn), (r_src, r_sgn), (k_src, k_sgn), (v_src, v_sgn)


def _relayout_cols(w, src_sgn):
    src, sgn = src_sgn
    return jnp.take(w, jnp.asarray(src), axis=1) * jnp.asarray(sgn)[None, :]


def _layer_norm(v, g, b):
    mu = jnp.mean(v, axis=-1, keepdims=True)
    c = v - mu
    var = jnp.mean(c * c, axis=-1, keepdims=True)
    return c * lax.rsqrt(var + NORM_EPS) * g + b


def _cparams(*sem):
    return pltpu.CompilerParams(dimension_semantics=sem, vmem_limit_bytes=VMEM_LIMIT)


def _rope_kernel(pos_ref, inv_ref, cos_ref, sin_ref):
    ang = pos_ref[...].astype(f32) * inv_ref[...]
    cos_ref[...] = jnp.cos(ang)
    sin_ref[...] = jnp.sin(ang)


def rope_tables(positions, tm=1024):
    t = positions.size
    half = MLA_ROPE // 2
    inv = ROPE_THETA ** (-jnp.arange(half, dtype=f32) / half)
    inv_row = jnp.zeros((1, LANES), f32)
    inv_row = inv_row.at[0, ROPE_LANE0:ROPE_LANE0 + half].set(inv)
    inv_row = inv_row.at[0, ROPE_LANE0 + half:ROPE_LANE0 + 2 * half].set(inv)
    tm = min(tm, t)
    return pl.pallas_call(
        _rope_kernel,
        grid=(t // tm,),
        in_specs=[pl.BlockSpec((tm, 1), lambda i: (i, 0)), pl.BlockSpec((1, LANES), lambda i: (0, 0))],
        out_specs=[pl.BlockSpec((tm, LANES), lambda i: (i, 0))] * 2,
        out_shape=[jax.ShapeDtypeStruct((t, LANES), f32)] * 2,
        compiler_params=_cparams("arbitrary"),
        name="rope_tables",
    )(positions.reshape(t, 1), inv_row)


def _inproj_kernel(x_ref, w_ref, sc_ref, cf_ref, mc_ref, kr_ref, q_ref, k_ref, v_ref):
    xb = x_ref[...].astype(bf16)
    outs = (sc_ref, cf_ref, mc_ref, kr_ref, q_ref, k_ref, v_ref)
    for j, o_ref in enumerate(outs):
        acc = jnp.dot(xb, w_ref[:, SEG_OFF[j]:SEG_OFF[j + 1]], preferred_element_type=f32)
        o_ref[...] = acc.astype(o_ref.dtype)


def inproj(x2d, w_cat, tm=512):
    t = x2d.shape[0]
    tm = min(tm, t)
    dts = (f32, f32, f32, f32, bf16, bf16, bf16)
    return pl.pallas_call(
        _inproj_kernel,
        grid=(t // tm,),
        in_specs=[pl.BlockSpec((tm, D_MODEL), lambda i: (i, 0)),
                  pl.BlockSpec((D_MODEL, IN_COLS_PADDED), lambda i: (0, 0))],
        out_specs=[pl.BlockSpec((tm, w), lambda i: (i, 0)) for w in SEGS],
        out_shape=[jax.ShapeDtypeStruct((t, w), d) for w, d in zip(SEGS, dts)],
        compiler_params=_cparams("arbitrary"),
        name="inproj",
    )(x2d, w_cat)


CONV_ROWS = 64
SC_HALO = 8
CF_HALO = 32


def _local_mix_kernel(sc_ref, sch_ref, cf_ref, cfh_ref, scw_ref, cfw_ref, cfb_ref, lng_ref, lnb_ref,
                      ysc_ref, ycf_ref, pad_sc, pad_cf):
    ts = sc_ref.shape[0]
    first = pl.program_id(1) == 0
    g = GROUP

    pad_sc[SC_HALO:SC_HALO + ts, :] = sc_ref[:, g:2 * g] * sc_ref[:, 2 * g:3 * g]
    halo = sch_ref[:, g:2 * g] * sch_ref[:, 2 * g:3 * g]
    pad_sc[0:SC_HALO, :] = jnp.where(first, 0.0, halo)
    for c in range(ts // CONV_ROWS):
        r0 = c * CONV_ROWS
        acc = jnp.zeros((CONV_ROWS, g), f32)
        for k in range(SC_KERNEL):
            o = SC_HALO + r0 - (SC_KERNEL - 1) + k
            acc = acc + scw_ref[k:k + 1, :] * pad_sc[o:o + CONV_ROWS, :]
        ysc_ref[r0:r0 + CONV_ROWS, :] = sc_ref[r0:r0 + CONV_ROWS, 0:g] * acc

    pad_cf[CF_HALO:CF_HALO + ts, :] = cf_ref[:, 0:g] * jax.nn.sigmoid(cf_ref[:, g:2 * g])
    halo = cfh_ref[:, 0:g] * jax.nn.sigmoid(cfh_ref[:, g:2 * g])
    pad_cf[0:CF_HALO, :] = jnp.where(first, 0.0, halo)
    for c in range(ts // CONV_ROWS):
        r0 = c * CONV_ROWS
        acc = jnp.zeros((CONV_ROWS, g), f32) + cfb_ref[...]
        for k in range(CF_KERNEL):
            o = CF_HALO + r0 - (CF_KERNEL - 1) + k
            acc = acc + cfw_ref[k:k + 1, :] * pad_cf[o:o + CONV_ROWS, :]
        u = _layer_norm(acc, lng_ref[...], lnb_ref[...])
        ycf_ref[r0:r0 + CONV_ROWS, :] = u * jax.nn.sigmoid(u)


def local_mix(sc, cf, sc_w, cf_w, cf_b, ln_g, ln_b, batch, seq, ts=512):
    t = sc.shape[0]
    ts = min(ts, seq)
    nsb = seq // ts
    main = lambda b, n: (b * nsb + n, 0)

    def halo(rows):
        per = ts // rows
        return lambda b, n: (jnp.maximum((b * nsb + n) * per - 1, 0), 0)

    row = lambda w: pl.BlockSpec((1, w), lambda b, n: (0, 0))
    return pl.pallas_call(
        _local_mix_kernel,
        grid=(batch, nsb),
        in_specs=[pl.BlockSpec((ts, SEG_SC), main), pl.BlockSpec((SC_HALO, SEG_SC), halo(SC_HALO)),
                  pl.BlockSpec((ts, SEG_CF), main), pl.BlockSpec((CF_HALO, SEG_CF), halo(CF_HALO)),
                  pl.BlockSpec((SC_KERNEL, GROUP), lambda b, n: (0, 0)),
                  pl.BlockSpec((CF_KERNEL, GROUP), lambda b, n: (0, 0)),
                  row(GROUP), row(GROUP), row(GROUP)],
        out_specs=[pl.BlockSpec((ts, GROUP), main)] * 2,
        out_shape=[jax.ShapeDtypeStruct((t, GROUP), f32)] * 2,
        scratch_shapes=[pltpu.VMEM((SC_HALO + ts, GROUP), f32), pltpu.VMEM((CF_HALO + ts, GROUP), f32)],
        compiler_params=_cparams("arbitrary", "arbitrary"),
        name="local_mix",
    )(sc, sc, cf, cf, sc_w, cf_w, cf_b.reshape(1, GROUP), ln_g.reshape(1, GROUP), ln_b.reshape(1, GROUP))


def _swa_kernel(sinks_ref, q_ref, kc_ref, kp_ref, vc_ref, vp_ref, o_ref):
    n = pl.program_id(1)
    w = WINDOW
    kk = jnp.concatenate([kp_ref[...], kc_ref[...]], axis=0)
    vv = jnp.concatenate([vp_ref[...], vc_ref[...]], axis=0)
    qi = lax.broadcasted_iota(jnp.int32, (w, 2 * w), 0)
    kj = lax.broadcasted_iota(jnp.int32, (w, 2 * w), 1)
    dist = qi + w - kj
    valid = (dist >= 0) & (dist < w) & ((n > 0) | (kj >= w))
    distf = dist.astype(f32)
    group = SWA_Q_HEADS // SWA_KV_HEADS
    for g in range(SWA_KV_HEADS):
        acc = jnp.zeros((w, LANES), f32)
        for gi in range(group):
            hq = g * group + gi
            slope = 2.0 ** (-8.0 * (hq + 1) / SWA_Q_HEADS)
            s = lax.dot_general(q_ref[:, hq * LANES:(hq + 1) * LANES], kk[:, g * LANES:(g + 1) * LANES],
                                (((1,), (1,)), ((), ())), preferred_element_type=f32)
            s = s * (HEAD ** -0.5) - slope * distf
            s = jnp.where(valid, s, -jnp.inf)
            sink = sinks_ref[hq]
            m = jnp.maximum(jnp.max(s, axis=-1, keepdims=True), sink)
            e = jnp.exp(s - m)
            den = jnp.sum(e, axis=-1, keepdims=True) + jnp.exp(sink - m)
            p = (e / den).astype(bf16)
            blk = 2 * g + gi
            acc = acc + jnp.dot(p, vv[:, blk * LANES:(blk + 1) * LANES], preferred_element_type=f32)
        o_ref[:, g * LANES:(g + 1) * LANES] = acc


def swa(q, k, v, sinks, batch, seq):
    t = q.shape[0]
    nb = seq // WINDOW
    cur = lambda b, n: (b * nb + n, 0)
    prev = lambda b, n: (b * nb + jnp.maximum(n - 1, 0), 0)
    return pl.pallas_call(
        _swa_kernel,
        grid=(batch, nb),
        in_specs=[pl.BlockSpec(memory_space=pltpu.SMEM),
                  pl.BlockSpec((WINDOW, SEG_SWQ), cur),
                  pl.BlockSpec((WINDOW, SEG_SWK), cur), pl.BlockSpec((WINDOW, SEG_SWK), prev),
                  pl.BlockSpec((WINDOW, SEG_SWV), cur), pl.BlockSpec((WINDOW, SEG_SWV), prev)],
        out_specs=pl.BlockSpec((WINDOW, GROUP), cur),
        out_shape=jax.ShapeDtypeStruct((t, GROUP), f32),
        compiler_params=_cparams("arbitrary", "arbitrary"),
        name="swa",
    )(sinks, q, k, k, v, v)


def _mla_prep_kernel(mc_ref, kr_ref, cos_ref, sin_ref, qg_ref, kg_ref, wq_ref, wqr_ref, wk_ref, wv_ref,
                     q_ref, k_ref, v_ref):
    cq = mc_ref[:, 0:MLA_Q_RANK]
    ckv = mc_ref[:, MLA_Q_RANK:MLA_Q_RANK + MLA_KV_RANK]
    cqn = (cq * lax.rsqrt(jnp.mean(cq * cq, axis=-1, keepdims=True) + NORM_EPS) * qg_ref[...]).astype(bf16)
    ckvn = (ckv * lax.rsqrt(jnp.mean(ckv * ckv, axis=-1, keepdims=True) + NORM_EPS) * kg_ref[...]).astype(bf16)
    cos = cos_ref[...]
    sin = sin_ref[...]
    k_rope = kr_ref[:, 0:LANES] * cos + kr_ref[:, LANES:2 * LANES] * sin
    v_ref[...] = jnp.dot(ckvn, wv_ref[...], preferred_element_type=f32).astype(bf16)
    for h in range(MLA_HEADS):
        sl = slice(h * LANES, (h + 1) * LANES)
        qh = jnp.dot(cqn, wq_ref[:, sl], preferred_element_type=f32)
        qr = jnp.dot(cqn, wqr_ref[:, sl], preferred_element_type=f32)
        q_ref[:, sl] = (qh * cos + qr * sin).astype(bf16)
        kh = jnp.dot(ckvn, wk_ref[:, sl], preferred_element_type=f32)
        k_ref[:, sl] = (kh + k_rope).astype(bf16)


def mla_prep(mc, kr, cos_t, sin_t, q_gain, kv_gain, wq, wqr, wk, wv, tm=512):
    t = mc.shape[0]
    tm = min(tm, t)
    n = MLA_HEADS * LANES
    tok = lambda w: pl.BlockSpec((tm, w), lambda i: (i, 0))
    full = lambda r, c: pl.BlockSpec((r, c), lambda i: (0, 0))
    return pl.pallas_call(
        _mla_prep_kernel,
        grid=(t // tm,),
        in_specs=[tok(SEG_MC), tok(SEG_KR), tok(LANES), tok(LANES),
                  full(1, MLA_Q_RANK), full(1, MLA_KV_RANK),
                  full(MLA_Q_RANK, n), full(MLA_Q_RANK, n), full(MLA_KV_RANK, n), full(MLA_KV_RANK, n)],
        out_specs=[tok(n)] * 3,
        out_shape=[jax.ShapeDtypeStruct((t, n), bf16)] * 3,
        compiler_params=_cparams("arbitrary"),
        name="mla_prep",
    )(mc, kr, cos_t, sin_t, q_gain.reshape(1, -1), kv_gain.reshape(1, -1), wq, wqr, wk, wv)


def _mla_attn_kernel(q_ref, k_ref, v_ref, o_ref):
    n = pl.program_id(1)
    qb = q_ref.shape[0]
    scale = (MLA_NOPE + MLA_ROPE) ** -0.5
    q_idx = n * qb + lax.broadcasted_iota(jnp.int32, (qb, qb), 0)
    lane = lax.broadcasted_iota(jnp.int32, (qb, LANES), 1)
    low = lane < HEAD
    npair = MLA_HEADS // 2

    def chunk(j, carry):
        ms, ls, accs = carry
        r0 = pl.multiple_of(j * qb, qb)
        k_idx = j * qb + lax.broadcasted_iota(jnp.int32, (qb, qb), 1)
        causal = k_idx <= q_idx
        new_ms, new_ls, new_accs = [], [], []
        for pr in range(npair):
            acc = accs[pr]
            alphas, pvs = [], []
            for hh in range(2):
                h = 2 * pr + hh
                sl = slice(h * LANES, (h + 1) * LANES)
                s = lax.dot_general(q_ref[:, sl], k_ref[pl.ds(r0, qb), sl],
                                    (((1,), (1,)), ((), ())), preferred_element_type=f32) * scale
                s = jnp.where(causal, s, -jnp.inf)
                m_new = jnp.maximum(ms[h], jnp.max(s, axis=-1, keepdims=True))
                alpha = jnp.exp(ms[h] - m_new)
                e = jnp.exp(s - m_new)
                new_ls.append(alpha * ls[h] + jnp.sum(e, axis=-1, keepdims=True))
                new_ms.append(m_new)
                alphas.append(alpha)
                pvs.append(jnp.dot(e.astype(bf16), v_ref[pl.ds(r0, qb), sl], preferred_element_type=f32))
            acc = acc * jnp.where(low, alphas[0], alphas[1]) + pvs[0] + pvs[1]
            new_accs.append(acc)
        return tuple(new_ms), tuple(new_ls), tuple(new_accs)

    init = (tuple(jnp.full((qb, 1), -jnp.inf, f32) for _ in range(MLA_HEADS)),
            tuple(jnp.zeros((qb, 1), f32) for _ in range(MLA_HEADS)),
            tuple(jnp.zeros((qb, LANES), f32) for _ in range(npair)))
    ms, ls, accs = lax.fori_loop(0, n + 1, chunk, init)
    for pr in range(npair):
        den = jnp.where(low, ls[2 * pr], ls[2 * pr + 1])
        o_ref[:, pr * LANES:(pr + 1) * LANES] = accs[pr] / den


def mla_attn(q, k, v, batch, seq, qb=256):
    t = q.shape[0]
    qb = min(qb, seq)
    nq = seq // qb
    n = MLA_HEADS * LANES
    return pl.pallas_call(
        _mla_attn_kernel,
        grid=(batch, nq),
        in_specs=[pl.BlockSpec((qb, n), lambda b, i: (b * nq + i, 0)),
                  pl.BlockSpec((seq, n), lambda b, i: (b, 0)),
                  pl.BlockSpec((seq, n), lambda b, i: (b, 0))],
        out_specs=pl.BlockSpec((qb, GROUP), lambda b, i: (b * nq + i, 0)),
        out_shape=jax.ShapeDtypeStruct((t, GROUP), f32),
        compiler_params=_cparams("arbitrary", "arbitrary"),
        name="mla_attn",
    )(q, k, v)


def _outproj_kernel(ysc_ref, ymla_ref, ycf_ref, ysw_ref, x_ref, mg_ref, w_ref, g_ref, b_ref, h_ref):
    tm = x_ref.shape[0]
    low = lax.broadcasted_iota(jnp.int32, (tm, LANES), 1) < HEAD
    acc = ALPHA * x_ref[...]
    for j, y_ref in enumerate((ysc_ref, ymla_ref, ycf_ref, ysw_ref)):
        for c in range(GROUP // LANES):
            col = j * GROUP + c * LANES
            y = y_ref[:, c * LANES:(c + 1) * LANES]
            y2 = y * y
            s_lo = jnp.sum(jnp.where(low, y2, 0.0), axis=-1, keepdims=True)
            s_hi = jnp.sum(jnp.where(low, 0.0, y2), axis=-1, keepdims=True)
            ms = jnp.where(low, s_lo, s_hi) * (1.0 / HEAD)
            yn = (y * lax.rsqrt(ms + NORM_EPS) * mg_ref[:, col:col + LANES]).astype(bf16)
            acc = acc + jnp.dot(yn, w_ref[col:col + LANES, :], preferred_element_type=f32)
    h_ref[...] = _layer_norm(acc, g_ref[...], b_ref[...])


def outproj(ysc, ymla, ycf, ysw, x2d, mix_g, w_out, ln_g, ln_b, tm=512):
    t = x2d.shape[0]
    tm = min(tm, t)
    tok = lambda w: pl.BlockSpec((tm, w), lambda i: (i, 0))
    full = lambda r, c: pl.BlockSpec((r, c), lambda i: (0, 0))
    return pl.pallas_call(
        _outproj_kernel,
        grid=(t // tm,),
        in_specs=[tok(GROUP)] * 4 + [tok(D_MODEL), full(1, D_MODEL), full(D_MODEL, D_MODEL),
                                     full(1, D_MODEL), full(1, D_MODEL)],
        out_specs=tok(D_MODEL),
        out_shape=jax.ShapeDtypeStruct((t, D_MODEL), f32),
        compiler_params=_cparams("arbitrary"),
        name="outproj",
    )(ysc, ymla, ycf, ysw, x2d, mix_g.reshape(1, -1), w_out, ln_g.reshape(1, -1), ln_b.reshape(1, -1))


def _topk_rows(s, payload, k):
    rows = s.shape[0]
    rid = lax.broadcasted_iota(jnp.int32, s.shape, 0)
    vals, pays = [], []
    for _ in range(k):
        m = jnp.max(s, axis=0, keepdims=True)
        first = jnp.min(jnp.where(s == m, rid, rows), axis=0, keepdims=True)
        hit = rid == first
        vals.append(m)
        pays.append(first if payload is None else jnp.sum(jnp.where(hit, payload, 0), axis=0, keepdims=True))
        s = jnp.where(hit, -jnp.inf, s)
    return jnp.concatenate(vals, axis=0), jnp.concatenate(pays, axis=0)


def _pair_candidates(a, b, combine):
    tm = a.shape[1]
    low4 = lax.broadcasted_iota(jnp.int32, (SUBLANES, tm), 0) < 4
    row = lambda v, i: jnp.broadcast_to(v[i:i + 1, :], (SUBLANES, tm))
    two = lambda v, i: jnp.where(low4, row(v, i), row(v, i + 1))
    b8 = b[0:SUBLANES]
    b44 = jnp.where(low4, b8, pltpu.roll(b8, 4, axis=0))
    firsts = [row(a, 0), row(a, 0), row(a, 1), row(a, 2), row(a, 3), two(a, 4), two(a, 6), a[SUBLANES:2 * SUBLANES]]
    seconds = [b8, b[SUBLANES:2 * SUBLANES], b8, b8, b8, b44, b44, row(b, 0)]
    return jnp.concatenate([combine(x, y) for x, y in zip(firsts, seconds)], axis=0)


def _peer_route_kernel(h_ref, wq_ref, keys_ref, idx_ref, gate_ref):
    k = PEER_TOPK
    q = jnp.dot(h_ref[...].astype(bf16), wq_ref[...], preferred_element_type=f32).astype(bf16)
    for h in range(PEER_HEADS):
        tops = []
        for p in range(2):
            hp = 2 * h + p
            st = lax.dot_general(keys_ref[hp], q[:, hp * PEER_HALF:(hp + 1) * PEER_HALF],
                                 (((1,), (1,)), ((), ())), preferred_element_type=f32)
            tops.append(_topk_rows(st, None, k))
        (a, ia), (b, ib) = tops
        cand_s = _pair_candidates(a, b, lambda x, y: x + y)
        cand_i = _pair_candidates(ia, ib, lambda x, y: x * PEER_NKEYS + y)
        best_s, best_i = _topk_rows(cand_s, cand_i, k)
        e = jnp.exp(best_s - jnp.max(best_s, axis=0, keepdims=True))
        gate_ref[h * k:(h + 1) * k, :] = e / jnp.sum(e, axis=0, keepdims=True)
        idx_ref[h * k:(h + 1) * k, :] = best_i


def peer_route(h2d, wq, keys, tm=256):
    t = h2d.shape[0]
    tm = min(tm, t)
    nq = wq.shape[1]
    return pl.pallas_call(
        _peer_route_kernel,
        grid=(t // tm,),
        in_specs=[pl.BlockSpec((tm, D_MODEL), lambda i: (i, 0)),
                  pl.BlockSpec((D_MODEL, nq), lambda i: (0, 0)),
                  pl.BlockSpec(keys.shape, lambda i: (0, 0, 0))],
        out_specs=[pl.BlockSpec((PEER_SLOTS, tm), lambda i: (0, i))] * 2,
        out_shape=[jax.ShapeDtypeStruct((PEER_SLOTS, t), jnp.int32), jax.ShapeDtypeStruct((PEER_SLOTS, t), f32)],
        compiler_params=_cparams("arbitrary"),
        name="peer_route",
    )(h2d, wq, keys)


PEER_TILE = 128
PEER_NBUF = 8
D_CHUNKS = D_MODEL // LANES
ISSUE_GROUP = PEER_SLOTS // (2 * D_CHUNKS)


def _peer_apply_kernel(idx_hbm, gate_ref, h_ref, tab_hbm, g_ref, b_ref, o_ref,
                       idx_smem, gbuf, ybuf, isem, gsem):
    tt = h_ref.shape[0]
    nb = PEER_NBUF
    step = pl.program_id(0)
    last = pl.num_programs(0) - 1
    cur = step % 2
    nxt_step = jnp.minimum(step + 1, last)

    def idx_copy(src_step, half):
        return pltpu.make_async_copy(idx_hbm.at[src_step], idx_smem.at[half], isem.at[half])

    def issue(half, t, buf, slots):
        for s in slots:
            pltpu.make_async_copy(tab_hbm.at[idx_smem[half, t, s]], gbuf.at[buf, :, s, :],
                                  gsem.at[buf]).start(priority=s % 2)

    def wait(buf):
        pltpu.make_async_copy(gbuf.at[buf], gbuf.at[buf], gsem.at[buf]).wait()

    @pl.when(step == 0)
    def _():
        first = idx_copy(0, 0)
        first.start()
        first.wait()
        for t0 in range(nb - 1):
            issue(0, t0, t0, range(PEER_SLOTS))

    idx_copy(nxt_step, 1 - cur).start()
    lane = lax.broadcasted_iota(jnp.int32, (PEER_SLOTS, tt), 1)

    def token(t, buf, src_half, src_t):
        nbuf = (buf + nb - 1) % nb
        groups = [range(g * ISSUE_GROUP, (g + 1) * ISSUE_GROUP) for g in range(2 * D_CHUNKS)]
        wait(buf)
        xt = h_ref[pl.ds(t, 1), :]
        acc = None
        for c in range(D_CHUNKS):
            issue(src_half, src_t, nbuf, groups[c])
            term = gbuf[buf, c] * xt[:, c * LANES:(c + 1) * LANES]
            acc = term if acc is None else acc + term
        s = jnp.sum(acc, axis=-1, keepdims=True)
        act = 0.5 * s * (1.0 + lax.erf(s * (2.0 ** -0.5)))
        gate = jnp.sum(jnp.where(lane == t, gate_ref[...], 0.0), axis=-1, keepdims=True)
        w = gate * act
        ys = []
        for c in range(D_CHUNKS):
            issue(src_half, src_t, nbuf, groups[D_CHUNKS + c])
            ys.append(jnp.sum(gbuf[buf, D_CHUNKS + c] * w, axis=0, keepdims=True))
        ybuf[pl.ds(t, 1), :] = jnp.concatenate(ys, axis=-1)

    def group(gi, carry):
        for b in range(nb):
            token(gi * nb + b, b, cur, gi * nb + b + nb - 1)
        return carry

    ngroups = tt // nb
    lax.fori_loop(0, ngroups - 1, group, 0)
    token(tt - nb, 0, cur, tt - 1)
    idx_copy(nxt_step, 1 - cur).wait()
    for b in range(1, nb):
        token(tt - nb + b, b, 1 - cur, b - 1)

    @pl.when(step == last)
    def _():
        for b in range(nb - 1):
            wait(b)

    o_ref[...] = _layer_norm(ALPHA * h_ref[...] + ybuf[...], g_ref[...], b_ref[...])


def peer_apply(idx_t, gate_t, h2d, table, ln_g, ln_b):
    t = h2d.shape[0]
    tt = min(PEER_TILE, t)
    assert tt % PEER_NBUF == 0 and tt >= 2 * PEER_NBUF
    idx = idx_t.T.reshape(t // tt, tt, PEER_SLOTS)
    row = pl.BlockSpec((1, D_MODEL), lambda i: (0, 0))
    return pl.pallas_call(
        _peer_apply_kernel,
        grid=(t // tt,),
        in_specs=[pl.BlockSpec(memory_space=pl.ANY),
                  pl.BlockSpec((PEER_SLOTS, tt), lambda i: (0, i)),
                  pl.BlockSpec((tt, D_MODEL), lambda i: (i, 0)),
                  pl.BlockSpec(memory_space=pl.ANY), row, row],
        out_specs=pl.BlockSpec((tt, D_MODEL), lambda i: (i, 0)),
        out_shape=jax.ShapeDtypeStruct((t, D_MODEL), f32),
        scratch_shapes=[pltpu.SMEM((2, tt, PEER_SLOTS), jnp.int32),
                        pltpu.VMEM((PEER_NBUF, 2 * D_CHUNKS, PEER_SLOTS, LANES), f32),
                        pltpu.VMEM((tt, D_MODEL), f32),
                        pltpu.SemaphoreType.DMA((2,)),
                        pltpu.SemaphoreType.DMA((PEER_NBUF,))],
        compiler_params=_cparams("arbitrary"),
        name="peer_apply",
    )(idx, gate_t, h2d, table, ln_g.reshape(1, -1), ln_b.reshape(1, -1))


def kernel(x, positions, w_in, sc_conv_w, mla_q_norm, mla_kv_norm, mla_w_uq, mla_w_uk, mla_w_uv,
           cf_dw_w, cf_dw_b, cf_ln_g, cf_ln_b, swa_sinks, mix_norm_g, w_out, ln1_g, ln1_b,
           peer_w_q, peer_sub_keys, peer_u, peer_v, ln2_g, ln2_b):
    batch, seq, d = x.shape
    t = batch * seq
    depth = w_in.shape[0]
    in_map = _inproj_column_map()
    q_map, qr_map, k_map, v_map = _mla_weight_maps()
    cos_t, sin_t = rope_tables(positions)
    xc = x.reshape(t, d)
    for l in range(depth):
        w_cat = _relayout_cols(w_in[l], in_map).astype(bf16)
        sc, cf, mc, kr, swq, swk, swv = inproj(xc, w_cat)
        y_sc, y_cf = local_mix(sc, cf, sc_conv_w[l], cf_dw_w[l], cf_dw_b[l], cf_ln_g[l], cf_ln_b[l], batch, seq)
        y_sw = swa(swq, swk, swv, swa_sinks[l], batch, seq)
        mq, mk, mv = mla_prep(mc, kr, cos_t, sin_t, mla_q_norm[l], mla_kv_norm[l],
                              _relayout_cols(mla_w_uq[l], q_map).astype(bf16),
                              _relayout_cols(mla_w_uq[l], qr_map).astype(bf16),
                              _relayout_cols(mla_w_uk[l], k_map).astype(bf16),
                              _relayout_cols(mla_w_uv[l], v_map).astype(bf16))
        y_mla = mla_attn(mq, mk, mv, batch, seq)
        h = outproj(y_sc, y_mla, y_cf, y_sw, xc, mix_norm_g[l], w_out[l].astype(bf16), ln1_g[l], ln1_b[l])
        keys = peer_sub_keys[l].reshape(2 * PEER_HEADS, PEER_NKEYS, PEER_HALF).astype(bf16)
        idx_t, gate_t = peer_route(h, peer_w_q[l].astype(bf16), keys)
        experts = peer_u.shape[1]
        table = jnp.concatenate([peer_u[l].reshape(experts, D_CHUNKS, LANES),
                                 peer_v[l].reshape(experts, D_CHUNKS, LANES)], axis=1)
        xc = peer_apply(idx_t, gate_t, h, table, ln2_g[l], ln2_b[l])
    return xc.reshape(batch, seq, d)
```

```python
import functools
import math

import numpy as np
import jax
import jax.numpy as jnp
from jax import lax
from jax.experimental import pallas as pl
from jax.experimental.pallas import tpu as pltpu

f32 = jnp.float32
bf16 = jnp.bfloat16

D_MODEL = 1024
DEPTH = 4
GROUP = 256
HEAD = 64
SC_KERNEL = 3
MLA_HEADS = 4
MLA_NOPE = 64
MLA_ROPE = 32
MLA_Q_RANK = 256
MLA_KV_RANK = 128
ROPE_THETA = 10000.0
Q_BLOCK = 128
CF_KERNEL = 31
SWA_Q_HEADS = 4
SWA_KV_HEADS = 2
WINDOW = 128
PEER_HEADS = 8
PEER_NKEYS = 128
PEER_TOPK = 16
PEER_HALF = 128
PEER_SLOTS = PEER_HEADS * PEER_TOPK
ALPHA = (2 * DEPTH) ** 0.25
NORM_EPS = 1e-5

LANES = 128
SUBLANES = 8
VMEM_LIMIT = 48 * 1024 * 1024

SEG_SC = 3 * GROUP
SEG_CF = 2 * GROUP
SEG_MC = MLA_Q_RANK + MLA_KV_RANK
SEG_KR = 2 * LANES
SEG_SWQ = SWA_Q_HEADS * LANES
SEG_SWK = SWA_KV_HEADS * LANES
SEG_SWV = 2 * SWA_KV_HEADS * LANES
SEGS = (SEG_SC, SEG_CF, SEG_MC, SEG_KR, SEG_SWQ, SEG_SWK, SEG_SWV)
SEG_OFF = tuple(int(v) for v in np.cumsum((0,) + SEGS))
IN_COLS_PADDED = SEG_OFF[-1]
ROPE_LANE0 = MLA_NOPE


def _inproj_column_map():
    o_sc, o_cq, o_ckv, o_kr, o_cf = 0, 768, 1024, 1152, 1184
    o_swq, o_swk, o_swv = 1696, 1952, 2080
    src = np.zeros(IN_COLS_PADDED, np.int32)
    sgn = np.zeros(IN_COLS_PADDED, np.float32)

    def put(dst, cols, sign=1.0):
        src[dst:dst + len(cols)] = cols
        sgn[dst:dst + len(cols)] = sign

    half = MLA_ROPE // 2
    put(SEG_OFF[0], o_sc + np.arange(SEG_SC))
    put(SEG_OFF[1], o_cf + np.arange(SEG_CF))
    put(SEG_OFF[2], o_cq + np.arange(SEG_MC))
    x1 = o_kr + np.arange(half)
    x2 = o_kr + half + np.arange(half)
    put(SEG_OFF[3] + ROPE_LANE0, x1)
    put(SEG_OFF[3] + ROPE_LANE0 + half, x2)
    put(SEG_OFF[3] + LANES + ROPE_LANE0, x2, -1.0)
    put(SEG_OFF[3] + LANES + ROPE_LANE0 + half, x1)
    for h in range(SWA_Q_HEADS):
        put(SEG_OFF[4] + h * LANES, o_swq + h * HEAD + np.arange(HEAD))
    for g in range(SWA_KV_HEADS):
        put(SEG_OFF[5] + g * LANES, o_swk + g * HEAD + np.arange(HEAD))
        put(SEG_OFF[6] + (2 * g) * LANES, o_swv + g * HEAD + np.arange(HEAD))
        put(SEG_OFF[6] + (2 * g + 1) * LANES + HEAD, o_swv + g * HEAD + np.arange(HEAD))
    return src, sgn


def _mla_weight_maps():
    half = MLA_ROPE // 2
    qd = MLA_NOPE + MLA_ROPE
    n = MLA_HEADS * LANES
    q_src = np.zeros(n, np.int32); q_sgn = np.zeros(n, np.float32)
    r_src = np.zeros(n, np.int32); r_sgn = np.zeros(n, np.float32)
    k_src = np.zeros(n, np.int32); k_sgn = np.zeros(n, np.float32)
    v_src = np.zeros(n, np.int32); v_sgn = np.zeros(n, np.float32)
    for h in range(MLA_HEADS):
        b = h * LANES
        q_src[b:b + qd] = h * qd + np.arange(qd); q_sgn[b:b + qd] = 1.0
        x1 = h * qd + MLA_NOPE + np.arange(half)
        x2 = x1 + half
        r_src[b + ROPE_LANE0:b + ROPE_LANE0 + half] = x2; r_sgn[b + ROPE_LANE0:b + ROPE_LANE0 + half] = -1.0
        r_src[b + ROPE_LANE0 + half:b + ROPE_LANE0 + 2 * half] = x1; r_sgn[b + ROPE_LANE0 + half:b + ROPE_LANE0 + 2 * half] = 1.0
        k_src[b:b + MLA_NOPE] = h * MLA_NOPE + np.arange(MLA_NOPE); k_sgn[b:b + MLA_NOPE] = 1.0
        lo = b + (HEAD if h % 2 else 0)
        v_src[lo:lo + HEAD] = h * HEAD + np.arange(HEAD); v_sgn[lo:lo + HEAD] = 1.0
    return (q_src, q_sgn), (r_src, r_sgn), (k_src, k_sgn), (v_src, v_sgn)


def _relayout_cols(w, src_sgn):
    src, sgn = src_sgn
    return jnp.take(w, jnp.asarray(src), axis=1) * jnp.asarray(sgn)[None, :]


def _layer_norm(v, g, b):
    mu = jnp.mean(v, axis=-1, keepdims=True)
    c = v - mu
    var = jnp.mean(c * c, axis=-1, keepdims=True)
    return c * lax.rsqrt(var + NORM_EPS) * g + b


def _cparams(*sem):
    return pltpu.CompilerParams(dimension_semantics=sem, vmem_limit_bytes=VMEM_LIMIT)


def _rope_kernel(pos_ref, inv_ref, cos_ref, sin_ref):
    ang = pos_ref[...].astype(f32) * inv_ref[...]
    cos_ref[...] = jnp.cos(ang)
    sin_ref[...] = jnp.sin(ang)


def rope_tables(positions, tm=1024):
    t = positions.size
    half = MLA_ROPE // 2
    inv = ROPE_THETA ** (-jnp.arange(half, dtype=f32) / half)
    inv_row = jnp.zeros((1, LANES), f32)
    inv_row = inv_row.at[0, ROPE_LANE0:ROPE_LANE0 + half].set(inv)
    inv_row = inv_row.at[0, ROPE_LANE0 + half:ROPE_LANE0 + 2 * half].set(inv)
    tm = min(tm, t)
    return pl.pallas_call(
        _rope_kernel,
        grid=(t // tm,),
        in_specs=[pl.BlockSpec((tm, 1), lambda i: (i, 0)), pl.BlockSpec((1, LANES), lambda i: (0, 0))],
        out_specs=[pl.BlockSpec((tm, LANES), lambda i: (i, 0))] * 2,
        out_shape=[jax.ShapeDtypeStruct((t, LANES), f32)] * 2,
        compiler_params=_cparams("arbitrary"),
        name="rope_tables",
    )(positions.reshape(t, 1), inv_row)


def _inproj_kernel(x_ref, w_ref, sc_ref, cf_ref, mc_ref, kr_ref, q_ref, k_ref, v_ref):
    xb = x_ref[...].astype(bf16)
    outs = (sc_ref, cf_ref, mc_ref, kr_ref, q_ref, k_ref, v_ref)
    for j, o_ref in enumerate(outs):
        acc = jnp.dot(xb, w_ref[:, SEG_OFF[j]:SEG_OFF[j + 1]], preferred_element_type=f32)
        o_ref[...] = acc.astype(o_ref.dtype)


def inproj(x2d, w_cat, tm=512):
    t = x2d.shape[0]
    tm = min(tm, t)
    dts = (f32, f32, f32, f32, bf16, bf16, bf16)
    return pl.pallas_call(
        _inproj_kernel,
        grid=(t // tm,),
        in_specs=[pl.BlockSpec((tm, D_MODEL), lambda i: (i, 0)),
                  pl.BlockSpec((D_MODEL, IN_COLS_PADDED), lambda i: (0, 0))],
        out_specs=[pl.BlockSpec((tm, w), lambda i: (i, 0)) for w in SEGS],
        out_shape=[jax.ShapeDtypeStruct((t, w), d) for w, d in zip(SEGS, dts)],
        compiler_params=_cparams("arbitrary"),
        name="inproj",
    )(x2d, w_cat)


CONV_ROWS = 64
SC_HALO = 8
CF_HALO = 32


def _local_mix_kernel(sc_ref, sch_ref, cf_ref, cfh_ref, scw_ref, cfw_ref, cfb_ref, lng_ref, lnb_ref,
                      ysc_ref, ycf_ref, pad_sc, pad_cf):
    ts = sc_ref.shape[0]
    first = pl.program_id(1) == 0
    g = GROUP

    pad_sc[SC_HALO:SC_HALO + ts, :] = sc_ref[:, g:2 * g] * sc_ref[:, 2 * g:3 * g]
    halo = sch_ref[:, g:2 * g] * sch_ref[:, 2 * g:3 * g]
    pad_sc[0:SC_HALO, :] = jnp.where(first, 0.0, halo)
    for c in range(ts // CONV_ROWS):
        r0 = c * CONV_ROWS
        acc = jnp.zeros((CONV_ROWS, g), f32)
        for k in range(SC_KERNEL):
            o = SC_HALO + r0 - (SC_KERNEL - 1) + k
            acc = acc + scw_ref[k:k + 1, :] * pad_sc[o:o + CONV_ROWS, :]
        ysc_ref[r0:r0 + CONV_ROWS, :] = sc_ref[r0:r0 + CONV_ROWS, 0:g] * acc

    pad_cf[CF_HALO:CF_HALO + ts, :] = cf_ref[:, 0:g] * jax.nn.sigmoid(cf_ref[:, g:2 * g])
    halo = cfh_ref[:, 0:g] * jax.nn.sigmoid(cfh_ref[:, g:2 * g])
    pad_cf[0:CF_HALO, :] = jnp.where(first, 0.0, halo)
    for c in range(ts // CONV_ROWS):
        r0 = c * CONV_ROWS
        acc = jnp.zeros((CONV_ROWS, g), f32) + cfb_ref[...]
        for k in range(CF_KERNEL):
            o = CF_HALO + r0 - (CF_KERNEL - 1) + k
            acc = acc + cfw_ref[k:k + 1, :] * pad_cf[o:o + CONV_ROWS, :]
        u = _layer_norm(acc, lng_ref[...], lnb_ref[...])
        ycf_ref[r0:r0 + CONV_ROWS, :] = u * jax.nn.sigmoid(u)


def local_mix(sc, cf, sc_w, cf_w, cf_b, ln_g, ln_b, batch, seq, ts=512):
    t = sc.shape[0]
    ts = min(ts, seq)
    nsb = seq // ts
    main = lambda b, n: (b * nsb + n, 0)

    def halo(rows):
        per = ts // rows
        return lambda b, n: (jnp.maximum((b * nsb + n) * per - 1, 0), 0)

    row = lambda w: pl.BlockSpec((1, w), lambda b, n: (0, 0))
    return pl.pallas_call(
        _local_mix_kernel,
        grid=(batch, nsb),
        in_specs=[pl.BlockSpec((ts, SEG_SC), main), pl.BlockSpec((SC_HALO, SEG_SC), halo(SC_HALO)),
                  pl.BlockSpec((ts, SEG_CF), main), pl.BlockSpec((CF_HALO, SEG_CF), halo(CF_HALO)),
                  pl.BlockSpec((SC_KERNEL, GROUP), lambda b, n: (0, 0)),
                  pl.BlockSpec((CF_KERNEL, GROUP), lambda b, n: (0, 0)),
                  row(GROUP), row(GROUP), row(GROUP)],
        out_specs=[pl.BlockSpec((ts, GROUP), main)] * 2,
        out_shape=[jax.ShapeDtypeStruct((t, GROUP), f32)] * 2,
        scratch_shapes=[pltpu.VMEM((SC_HALO + ts, GROUP), f32), pltpu.VMEM((CF_HALO + ts, GROUP), f32)],
        compiler_params=_cparams("arbitrary", "arbitrary"),
        name="local_mix",
    )(sc, sc, cf, cf, sc_w, cf_w, cf_b.reshape(1, GROUP), ln_g.reshape(1, GROUP), ln_b.reshape(1, GROUP))


def _swa_kernel(sinks_ref, q_ref, kc_ref, kp_ref, vc_ref, vp_ref, o_ref):
    n = pl.program_id(1)
    w = WINDOW
    kk = jnp.concatenate([kp_ref[...], kc_ref[...]], axis=0)
    vv = jnp.concatenate([vp_ref[...], vc_ref[...]], axis=0)
    qi = lax.broadcasted_iota(jnp.int32, (w, 2 * w), 0)
    kj = lax.broadcasted_iota(jnp.int32, (w, 2 * w), 1)
    dist = qi + w - kj
    valid = (dist >= 0) & (dist < w) & ((n > 0) | (kj >= w))
    distf = dist.astype(f32)
    group = SWA_Q_HEADS // SWA_KV_HEADS
    for g in range(SWA_KV_HEADS):
        acc = jnp.zeros((w, LANES), f32)
        for gi in range(group):
            hq = g * group + gi
            slope = 2.0 ** (-8.0 * (hq + 1) / SWA_Q_HEADS)
            s = lax.dot_general(q_ref[:, hq * LANES:(hq + 1) * LANES], kk[:, g * LANES:(g + 1) * LANES],
                                (((1,), (1,)), ((), ())), preferred_element_type=f32)
            s = s * (HEAD ** -0.5) - slope * distf
            s = jnp.where(valid, s, -jnp.inf)
            sink = sinks_ref[hq]
            m = jnp.maximum(jnp.max(s, axis=-1, keepdims=True), sink)
            e = jnp.exp(s - m)
            den = jnp.sum(e, axis=-1, keepdims=True) + jnp.exp(sink - m)
            p = (e / den).astype(bf16)
            blk = 2 * g + gi
            acc = acc + jnp.dot(p, vv[:, blk * LANES:(blk + 1) * LANES], preferred_element_type=f32)
        o_ref[:, g * LANES:(g + 1) * LANES] = acc


def swa(q, k, v, sinks, batch, seq):
    t = q.shape[0]
    nb = seq // WINDOW
    cur = lambda b, n: (b * nb + n, 0)
    prev = lambda b, n: (b * nb + jnp.maximum(n - 1, 0), 0)
    return pl.pallas_call(
        _swa_kernel,
        grid=(batch, nb),
        in_specs=[pl.BlockSpec(memory_space=pltpu.SMEM),
                  pl.BlockSpec((WINDOW, SEG_SWQ), cur),
                  pl.BlockSpec((WINDOW, SEG_SWK), cur), pl.BlockSpec((WINDOW, SEG_SWK), prev),
                  pl.BlockSpec((WINDOW, SEG_SWV), cur), pl.BlockSpec((WINDOW, SEG_SWV), prev)],
        out_specs=pl.BlockSpec((WINDOW, GROUP), cur),
        out_shape=jax.ShapeDtypeStruct((t, GROUP), f32),
        compiler_params=_cparams("arbitrary", "arbitrary"),
        name="swa",
    )(sinks, q, k, k, v, v)


def _mla_prep_kernel(mc_ref, kr_ref, cos_ref, sin_ref, qg_ref, kg_ref, wq_ref, wqr_ref, wk_ref, wv_ref,
                     q_ref, k_ref, v_ref):
    cq = mc_ref[:, 0:MLA_Q_RANK]
    ckv = mc_ref[:, MLA_Q_RANK:MLA_Q_RANK + MLA_KV_RANK]
    cqn = (cq * lax.rsqrt(jnp.mean(cq * cq, axis=-1, keepdims=True) + NORM_EPS) * qg_ref[...]).astype(bf16)
    ckvn = (ckv * lax.rsqrt(jnp.mean(ckv * ckv, axis=-1, keepdims=True) + NORM_EPS) * kg_ref[...]).astype(bf16)
    cos = cos_ref[...]
    sin = sin_ref[...]
    k_rope = kr_ref[:, 0:LANES] * cos + kr_ref[:, LANES:2 * LANES] * sin
    v_ref[...] = jnp.dot(ckvn, wv_ref[...], preferred_element_type=f32).astype(bf16)
    for h in range(MLA_HEADS):
        sl = slice(h * LANES, (h + 1) * LANES)
        qh = jnp.dot(cqn, wq_ref[:, sl], preferred_element_type=f32)
        qr = jnp.dot(cqn, wqr_ref[:, sl], preferred_element_type=f32)
        q_ref[:, sl] = (qh * cos + qr * sin).astype(bf16)
        kh = jnp.dot(ckvn, wk_ref[:, sl], preferred_element_type=f32)
        k_ref[:, sl] = (kh + k_rope).astype(bf16)


def mla_prep(mc, kr, cos_t, sin_t, q_gain, kv_gain, wq, wqr, wk, wv, tm=512):
    t = mc.shape[0]
    tm = min(tm, t)
    n = MLA_HEADS * LANES
    tok = lambda w: pl.BlockSpec((tm, w), lambda i: (i, 0))
    full = lambda r, c: pl.BlockSpec((r, c), lambda i: (0, 0))
    return pl.pallas_call(
        _mla_prep_kernel,
        grid=(t // tm,),
        in_specs=[tok(SEG_MC), tok(SEG_KR), tok(LANES), tok(LANES),
                  full(1, MLA_Q_RANK), full(1, MLA_KV_RANK),
                  full(MLA_Q_RANK, n), full(MLA_Q_RANK, n), full(MLA_KV_RANK, n), full(MLA_KV_RANK, n)],
        out_specs=[tok(n)] * 3,
        out_shape=[jax.ShapeDtypeStruct((t, n), bf16)] * 3,
        compiler_params=_cparams("arbitrary"),
        name="mla_prep",
    )(mc, kr, cos_t, sin_t, q_gain.reshape(1, -1), kv_gain.reshape(1, -1), wq, wqr, wk, wv)


def _mla_attn_kernel(q_ref, k_ref, v_ref, o_ref):
    n = pl.program_id(1)
    qb = q_ref.shape[0]
    scale = (MLA_NOPE + MLA_ROPE) ** -0.5
    q_idx = n * qb + lax.broadcasted_iota(jnp.int32, (qb, qb), 0)
    lane = lax.broadcasted_iota(jnp.int32, (qb, LANES), 1)
    low = lane < HEAD
    npair = MLA_HEADS // 2

    def chunk(j, carry):
        ms, ls, accs = carry
        r0 = pl.multiple_of(j * qb, qb)
        k_idx = j * qb + lax.broadcasted_iota(jnp.int32, (qb, qb), 1)
        causal = k_idx <= q_idx
        new_ms, new_ls, new_accs = [], [], []
        for pr in range(npair):
            acc = accs[pr]
            alphas, pvs = [], []
            for hh in range(2):
                h = 2 * pr + hh
                sl = slice(h * LANES, (h + 1) * LANES)
                s = lax.dot_general(q_ref[:, sl], k_ref[pl.ds(r0, qb), sl],
                                    (((1,), (1,)), ((), ())), preferred_element_type=f32) * scale
                s = jnp.where(causal, s, -jnp.inf)
                m_new = jnp.maximum(ms[h], jnp.max(s, axis=-1, keepdims=True))
                alpha = jnp.exp(ms[h] - m_new)
                e = jnp.exp(s - m_new)
                new_ls.append(alpha * ls[h] + jnp.sum(e, axis=-1, keepdims=True))
                new_ms.append(m_new)
                alphas.append(alpha)
                pvs.append(jnp.dot(e.astype(bf16), v_ref[pl.ds(r0, qb), sl], preferred_element_type=f32))
            acc = acc * jnp.where(low, alphas[0], alphas[1]) + pvs[0] + pvs[1]
            new_accs.append(acc)
        return tuple(new_ms), tuple(new_ls), tuple(new_accs)

    init = (tuple(jnp.full((qb, 1), -jnp.inf, f32) for _ in range(MLA_HEADS)),
            tuple(jnp.zeros((qb, 1), f32) for _ in range(MLA_HEADS)),
            tuple(jnp.zeros((qb, LANES), f32) for _ in range(npair)))
    ms, ls, accs = lax.fori_loop(0, n + 1, chunk, init)
    for pr in range(npair):
        den = jnp.where(low, ls[2 * pr], ls[2 * pr + 1])
        o_ref[:, pr * LANES:(pr + 1) * LANES] = accs[pr] / den


def mla_attn(q, k, v, batch, seq, qb=256):
    t = q.shape[0]
    qb = min(qb, seq)
    nq = seq // qb
    n = MLA_HEADS * LANES
    return pl.pallas_call(
        _mla_attn_kernel,
        grid=(batch, nq),
        in_specs=[pl.BlockSpec((qb, n), lambda b, i: (b * nq + i, 0)),
                  pl.BlockSpec((seq, n), lambda b, i: (b, 0)),
                  pl.BlockSpec((seq, n), lambda b, i: (b, 0))],
        out_specs=pl.BlockSpec((qb, GROUP), lambda b, i: (b * nq + i, 0)),
        out_shape=jax.ShapeDtypeStruct((t, GROUP), f32),
        compiler_params=_cparams("arbitrary", "arbitrary"),
        name="mla_attn",
    )(q, k, v)


def _outproj_kernel(ysc_ref, ymla_ref, ycf_ref, ysw_ref, x_ref, mg_ref, w_ref, g_ref, b_ref, h_ref):
    tm = x_ref.shape[0]
    low = lax.broadcasted_iota(jnp.int32, (tm, LANES), 1) < HEAD
    acc = ALPHA * x_ref[...]
    for j, y_ref in enumerate((ysc_ref, ymla_ref, ycf_ref, ysw_ref)):
        for c in range(GROUP // LANES):
            col = j * GROUP + c * LANES
            y = y_ref[:, c * LANES:(c + 1) * LANES]
            y2 = y * y
            s_lo = jnp.sum(jnp.where(low, y2, 0.0), axis=-1, keepdims=True)
            s_hi = jnp.sum(jnp.where(low, 0.0, y2), axis=-1, keepdims=True)
            ms = jnp.where(low, s_lo, s_hi) * (1.0 / HEAD)
            yn = (y * lax.rsqrt(ms + NORM_EPS) * mg_ref[:, col:col + LANES]).astype(bf16)
            acc = acc + jnp.dot(yn, w_ref[col:col + LANES, :], preferred_element_type=f32)
    h_ref[...] = _layer_norm(acc, g_ref[...], b_ref[...])


def outproj(ysc, ymla, ycf, ysw, x2d, mix_g, w_out, ln_g, ln_b, tm=512):
    t = x2d.shape[0]
    tm = min(tm, t)
    tok = lambda w: pl.BlockSpec((tm, w), lambda i: (i, 0))
    full = lambda r, c: pl.BlockSpec((r, c), lambda i: (0, 0))
    return pl.pallas_call(
        _outproj_kernel,
        grid=(t // tm,),
        in_specs=[tok(GROUP)] * 4 + [tok(D_MODEL), full(1, D_MODEL), full(D_MODEL, D_MODEL),
                                     full(1, D_MODEL), full(1, D_MODEL)],
        out_specs=tok(D_MODEL),
        out_shape=jax.ShapeDtypeStruct((t, D_MODEL), f32),
        compiler_params=_cparams("arbitrary"),
        name="outproj",
    )(ysc, ymla, ycf, ysw, x2d, mix_g.reshape(1, -1), w_out, ln_g.reshape(1, -1), ln_b.reshape(1, -1))


def _topk_rows(s, payload, k):
    rows = s.shape[0]
    rid = lax.broadcasted_iota(jnp.int32, s.shape, 0)
    vals, pays = [], []
    for _ in range(k):
        m = jnp.max(s, axis=0, keepdims=True)
        first = jnp.min(jnp.where(s == m, rid, rows), axis=0, keepdims=True)
        hit = rid == first
        vals.append(m)
        pays.append(first if payload is None else jnp.sum(jnp.where(hit, payload, 0), axis=0, keepdims=True))
        s = jnp.where(hit, -jnp.inf, s)
    return jnp.concatenate(vals, axis=0), jnp.concatenate(pays, axis=0)


def _pair_candidates(a, b, combine):
    tm = a.shape[1]
    low4 = lax.broadcasted_iota(jnp.int32, (SUBLANES, tm), 0) < 4
    row = lambda v, i: jnp.broadcast_to(v[i:i + 1, :], (SUBLANES, tm))
    two = lambda v, i: jnp.where(low4, row(v, i), row(v, i + 1))
    b8 = b[0:SUBLANES]
    b44 = jnp.where(low4, b8, pltpu.roll(b8, 4, axis=0))
    firsts = [row(a, 0), row(a, 0), row(a, 1), row(a, 2), row(a, 3), two(a, 4), two(a, 6), a[SUBLANES:2 * SUBLANES]]
    seconds = [b8, b[SUBLANES:2 * SUBLANES], b8, b8, b8, b44, b44, row(b, 0)]
    return jnp.concatenate([combine(x, y) for x, y in zip(firsts, seconds)], axis=0)


def _peer_route_kernel(h_ref, wq_ref, keys_ref, idx_ref, gate_ref):
    k = PEER_TOPK
    q = jnp.dot(h_ref[...].astype(bf16), wq_ref[...], preferred_element_type=f32).astype(bf16)
    for h in range(PEER_HEADS):
        tops = []
        for p in range(2):
            hp = 2 * h + p
            st = lax.dot_general(keys_ref[hp], q[:, hp * PEER_HALF:(hp + 1) * PEER_HALF],
                                 (((1,), (1,)), ((), ())), preferred_element_type=f32)
            tops.append(_topk_rows(st, None, k))
        (a, ia), (b, ib) = tops
        cand_s = _pair_candidates(a, b, lambda x, y: x + y)
        cand_i = _pair_candidates(ia, ib, lambda x, y: x * PEER_NKEYS + y)
        best_s, best_i = _topk_rows(cand_s, cand_i, k)
        e = jnp.exp(best_s - jnp.max(best_s, axis=0, keepdims=True))
        gate_ref[h * k:(h + 1) * k, :] = e / jnp.sum(e, axis=0, keepdims=True)
        idx_ref[h * k:(h + 1) * k, :] = best_i


def peer_route(h2d, wq, keys, tm=256):
    t = h2d.shape[0]
    tm = min(tm, t)
    nq = wq.shape[1]
    return pl.pallas_call(
        _peer_route_kernel,
        grid=(t // tm,),
        in_specs=[pl.BlockSpec((tm, D_MODEL), lambda i: (i, 0)),
                  pl.BlockSpec((D_MODEL, nq), lambda i: (0, 0)),
                  pl.BlockSpec(keys.shape, lambda i: (0, 0, 0))],
        out_specs=[pl.BlockSpec((PEER_SLOTS, tm), lambda i: (0, i))] * 2,
        out_shape=[jax.ShapeDtypeStruct((PEER_SLOTS, t), jnp.int32), jax.ShapeDtypeStruct((PEER_SLOTS, t), f32)],
        compiler_params=_cparams("arbitrary"),
        name="peer_route",
    )(h2d, wq, keys)


PEER_TILE = 128
PEER_NBUF = 16
PEER_GROUP = 4
D_CHUNKS = D_MODEL // LANES
SLAB_ROWS = 2 * D_CHUNKS
V_ACCS = 4


def _slab_layer_norm(z, g, b):
    def mean(v):
        return jnp.sum(jnp.sum(v, axis=2, keepdims=True), axis=1, keepdims=True) * (1.0 / D_MODEL)
    c = z - mean(z)
    return c * lax.rsqrt(mean(c * c) + NORM_EPS) * g + b


def _peer_apply_kernel(idx_hbm, gate_ref, h_ref, tab_hbm, g_ref, b_ref, o_ref,
                       idx_smem, gbuf, ybuf, wrep, isem, gsem):
    tt = h_ref.shape[0]
    nb, grp = PEER_NBUF, PEER_GROUP
    ahead = nb - grp
    step = pl.program_id(0)
    last = pl.num_programs(0) - 1
    cur = step % 2
    nxt_step = jnp.minimum(step + 1, last)

    def idx_copy(src_step, half):
        return pltpu.make_async_copy(idx_hbm.at[src_step], idx_smem.at[half], isem.at[half])

    def issue(half, t, buf, s):
        pltpu.make_async_copy(tab_hbm.at[idx_smem[half, t, s]], gbuf.at[buf, pl.ds(s * SLAB_ROWS, SLAB_ROWS), :],
                              gsem.at[buf]).start(priority=s % 2)

    def wait(buf):
        pltpu.make_async_copy(gbuf.at[buf], gbuf.at[buf], gsem.at[buf]).wait()

    @pl.when(step == 0)
    def _():
        first = idx_copy(0, 0)
        first.start()
        first.wait()
        for t0 in range(ahead):
            for s in range(PEER_SLOTS):
                issue(0, t0, t0, s)

    idx_copy(nxt_step, 1 - cur).start()
    lane = lax.broadcasted_iota(jnp.int32, (SUBLANES, LANES), 1)

    def v_step(buf, s, accs):
        v = gbuf[buf, s * SLAB_ROWS + D_CHUNKS:(s + 1) * SLAB_ROWS, :]
        accs[s % V_ACCS] = accs[s % V_ACCS] + v * wrep[buf % (2 * grp), s:s + 1, :]

    def u_step(buf, s, xt, dots):
        r = jnp.sum(gbuf[buf, s * SLAB_ROWS:s * SLAB_ROWS + D_CHUNKS, :] * xt, axis=-1, keepdims=True)
        return jnp.where(lane == s, r, dots)

    def weights(t, dots):
        sc = jnp.sum(dots, axis=0, keepdims=True)
        act = 0.5 * sc * (1.0 + lax.erf(sc * (2.0 ** -0.5)))
        w = gate_ref[pl.ds(t, 1), :] * act
        return jnp.broadcast_to(w, (PEER_SLOTS, LANES)).T

    def combine(buf):
        accs = [jnp.zeros((SUBLANES, LANES), f32) for _ in range(V_ACCS)]
        for s in range(PEER_SLOTS):
            v_step(buf, s, accs)
        return functools.reduce(lambda p, q: p + q, accs)

    def tokens(t0, b0, src_half, src_t0, lagging=True):
        p0 = (b0 + nb - grp) % nb
        for k in range(grp):
            wait(b0 + k)
        xts = [h_ref[t0 + k] for k in range(grp)]
        dots = [jnp.zeros((SUBLANES, LANES), f32) for _ in range(grp)]
        accs = [[jnp.zeros((SUBLANES, LANES), f32) for _ in range(V_ACCS)] for _ in range(grp)]
        for s in range(PEER_SLOTS):
            for k in range(grp):
                if lagging:
                    v_step(p0 + k, s, accs[k])
                dots[k] = u_step(b0 + k, s, xts[k], dots[k])
            for k in range(grp):
                issue(src_half, src_t0 + k, p0 + k, s)
        for k in range(grp):
            wrep[(b0 + k) % (2 * grp)] = weights(t0 + k, dots[k])
        if lagging:
            for k in range(grp):
                ybuf[t0 - grp + k] = functools.reduce(lambda p, q: p + q, accs[k])

    per_body = nb // grp
    nbodies = tt // nb

    def body(t0, source_of, lag_first=True):
        for j in range(per_body):
            half, src = source_of(j)
            tokens(t0 + j * grp, j * grp, half, src, lagging=lag_first or j > 0)

    body(0, lambda j: (cur, j * grp + ahead), lag_first=False)

    def rolled(bi, carry):
        body(bi * nb, lambda j: (cur, bi * nb + j * grp + ahead))
        return carry

    lax.fori_loop(1, nbodies - 1, rolled, 0)
    idx_copy(nxt_step, 1 - cur).wait()
    body(tt - nb, lambda j: (cur, tt - grp) if j == 0 else (1 - cur, (j - 1) * grp))
    for k in range(grp):
        ybuf[tt - grp + k] = combine(nb - grp + k)

    @pl.when(step == last)
    def _():
        for b in range(ahead):
            wait(b)

    o_ref[...] = _slab_layer_norm(ALPHA * h_ref[...] + ybuf[...], g_ref[...], b_ref[...])


def peer_apply(idx_t, gate_t, h2d, table, ln_g, ln_b):
    t = h2d.shape[0]
    tt = min(PEER_TILE, t)
    assert tt % PEER_NBUF == 0 and tt >= 3 * PEER_NBUF and PEER_NBUF % (2 * PEER_GROUP) == 0
    idx = idx_t.T.reshape(t // tt, tt, PEER_SLOTS)
    slab = pl.BlockSpec((D_CHUNKS, LANES), lambda i: (0, 0))
    tok = pl.BlockSpec((tt, D_CHUNKS, LANES), lambda i: (i, 0, 0))
    out = pl.pallas_call(
        _peer_apply_kernel,
        grid=(t // tt,),
        in_specs=[pl.BlockSpec(memory_space=pl.ANY),
                  pl.BlockSpec((tt, PEER_SLOTS), lambda i: (i, 0)),
                  tok,
                  pl.BlockSpec(memory_space=pl.ANY), slab, slab],
        out_specs=tok,
        out_shape=jax.ShapeDtypeStruct((t, D_CHUNKS, LANES), f32),
        scratch_shapes=[pltpu.SMEM((2, tt, PEER_SLOTS), jnp.int32),
                        pltpu.VMEM((PEER_NBUF, PEER_SLOTS * SLAB_ROWS, LANES), f32),
                        pltpu.VMEM((tt, D_CHUNKS, LANES), f32),
                        pltpu.VMEM((2 * PEER_GROUP, PEER_SLOTS, LANES), f32),
                        pltpu.SemaphoreType.DMA((2,)),
                        pltpu.SemaphoreType.DMA((PEER_NBUF,))],
        compiler_params=_cparams("arbitrary"),
        name="peer_apply",
    )(idx, gate_t.T, h2d.reshape(t, D_CHUNKS, LANES), table,
      ln_g.reshape(D_CHUNKS, LANES), ln_b.reshape(D_CHUNKS, LANES))
    return out.reshape(t, D_MODEL)


def kernel(x, positions, w_in, sc_conv_w, mla_q_norm, mla_kv_norm, mla_w_uq, mla_w_uk, mla_w_uv,
           cf_dw_w, cf_dw_b, cf_ln_g, cf_ln_b, swa_sinks, mix_norm_g, w_out, ln1_g, ln1_b,
           peer_w_q, peer_sub_keys, peer_u, peer_v, ln2_g, ln2_b):
    batch, seq, d = x.shape
    t = batch * seq
    depth = w_in.shape[0]
    in_map = _inproj_column_map()
    q_map, qr_map, k_map, v_map = _mla_weight_maps()
    cos_t, sin_t = rope_tables(positions)
    xc = x.reshape(t, d)
    for l in range(depth):
        w_cat = _relayout_cols(w_in[l], in_map).astype(bf16)
        sc, cf, mc, kr, swq, swk, swv = inproj(xc, w_cat)
        y_sc, y_cf = local_mix(sc, cf, sc_conv_w[l], cf_dw_w[l], cf_dw_b[l], cf_ln_g[l], cf_ln_b[l], batch, seq)
        y_sw = swa(swq, swk, swv, swa_sinks[l], batch, seq)
        mq, mk, mv = mla_prep(mc, kr, cos_t, sin_t, mla_q_norm[l], mla_kv_norm[l],
                              _relayout_cols(mla_w_uq[l], q_map).astype(bf16),
                              _relayout_cols(mla_w_uq[l], qr_map).astype(bf16),
                              _relayout_cols(mla_w_uk[l], k_map).astype(bf16),
                              _relayout_cols(mla_w_uv[l], v_map).astype(bf16))
        y_mla = mla_attn(mq, mk, mv, batch, seq)
        h = outproj(y_sc, y_mla, y_cf, y_sw, xc, mix_norm_g[l], w_out[l].astype(bf16), ln1_g[l], ln1_b[l])
        keys = peer_sub_keys[l].reshape(2 * PEER_HEADS, PEER_NKEYS, PEER_HALF).astype(bf16)
        idx_t, gate_t = peer_route(h, peer_w_q[l].astype(bf16), keys)
        experts = peer_u.shape[1]
        table = jnp.concatenate([peer_u[l].reshape(experts, D_CHUNKS, LANES),
                                 peer_v[l].reshape(experts, D_CHUNKS, LANES)], axis=1)
        xc = peer_apply(idx_t, gate_t, h, table, ln2_g[l], ln2_b[l])
    return xc.reshape(batch, seq, d)
```

```python
import functools
import math

import numpy as np
import jax
import jax.numpy as jnp
from jax import lax
from jax.experimental import pallas as pl
from jax.experimental.pallas import tpu as pltpu

f32 = jnp.float32
bf16 = jnp.bfloat16

D_MODEL = 1024
DEPTH = 4
GROUP = 256
HEAD = 64
SC_KERNEL = 3
MLA_HEADS = 4
MLA_NOPE = 64
MLA_ROPE = 32
MLA_Q_RANK = 256
MLA_KV_RANK = 128
ROPE_THETA = 10000.0
Q_BLOCK = 128
CF_KERNEL = 31
SWA_Q_HEADS = 4
SWA_KV_HEADS = 2
WINDOW = 128
PEER_HEADS = 8
PEER_NKEYS = 128
PEER_TOPK = 16
PEER_HALF = 128
PEER_SLOTS = PEER_HEADS * PEER_TOPK
ALPHA = (2 * DEPTH) ** 0.25
NORM_EPS = 1e-5

LANES = 128
SUBLANES = 8
VMEM_LIMIT = 48 * 1024 * 1024

SEG_SC = 3 * GROUP
SEG_CF = 2 * GROUP
SEG_MC = MLA_Q_RANK + MLA_KV_RANK
SEG_KR = 2 * LANES
SEG_SWQ = SWA_Q_HEADS * LANES
SEG_SWK = SWA_KV_HEADS * LANES
SEG_SWV = 2 * SWA_KV_HEADS * LANES
SEGS = (SEG_SC, SEG_CF, SEG_MC, SEG_KR, SEG_SWQ, SEG_SWK, SEG_SWV)
SEG_OFF = tuple(int(v) for v in np.cumsum((0,) + SEGS))
IN_COLS_PADDED = SEG_OFF[-1]
ROPE_LANE0 = MLA_NOPE


def _inproj_column_map():
    o_sc, o_cq, o_ckv, o_kr, o_cf = 0, 768, 1024, 1152, 1184
    o_swq, o_swk, o_swv = 1696, 1952, 2080
    src = np.zeros(IN_COLS_PADDED, np.int32)
    sgn = np.zeros(IN_COLS_PADDED, np.float32)

    def put(dst, cols, sign=1.0):
        src[dst:dst + len(cols)] = cols
        sgn[dst:dst + len(cols)] = sign

    half = MLA_ROPE // 2
    put(SEG_OFF[0], o_sc + np.arange(SEG_SC))
    put(SEG_OFF[1], o_cf + np.arange(SEG_CF))
    put(SEG_OFF[2], o_cq + np.arange(SEG_MC))
    x1 = o_kr + np.arange(half)
    x2 = o_kr + half + np.arange(half)
    put(SEG_OFF[3] + ROPE_LANE0, x1)
    put(SEG_OFF[3] + ROPE_LANE0 + half, x2)
    put(SEG_OFF[3] + LANES + ROPE_LANE0, x2, -1.0)
    put(SEG_OFF[3] + LANES + ROPE_LANE0 + half, x1)
    for h in range(SWA_Q_HEADS):
        put(SEG_OFF[4] + h * LANES, o_swq + h * HEAD + np.arange(HEAD))
    for g in range(SWA_KV_HEADS):
        put(SEG_OFF[5] + g * LANES, o_swk + g * HEAD + np.arange(HEAD))
        put(SEG_OFF[6] + (2 * g) * LANES, o_swv + g * HEAD + np.arange(HEAD))
        put(SEG_OFF[6] + (2 * g + 1) * LANES + HEAD, o_swv + g * HEAD + np.arange(HEAD))
    return src, sgn


def _mla_weight_maps():
    half = MLA_ROPE // 2
    qd = MLA_NOPE + MLA_ROPE
    n = MLA_HEADS * LANES
    q_src = np.zeros(n, np.int32); q_sgn = np.zeros(n, np.float32)
    r_src = np.zeros(n, np.int32); r_sgn = np.zeros(n, np.float32)
    k_src = np.zeros(n, np.int32); k_sgn = np.zeros(n, np.float32)
    v_src = np.zeros(n, np.int32); v_sgn = np.zeros(n, np.float32)
    for h in range(MLA_HEADS):
        b = h * LANES
        q_src[b:b + qd] = h * qd + np.arange(qd); q_sgn[b:b + qd] = 1.0
        x1 = h * qd + MLA_NOPE + np.arange(half)
        x2 = x1 + half
        r_src[b + ROPE_LANE0:b + ROPE_LANE0 + half] = x2; r_sgn[b + ROPE_LANE0:b + ROPE_LANE0 + half] = -1.0
        r_src[b + ROPE_LANE0 + half:b + ROPE_LANE0 + 2 * half] = x1; r_sgn[b + ROPE_LANE0 + half:b + ROPE_LANE0 + 2 * half] = 1.0
        k_src[b:b + MLA_NOPE] = h * MLA_NOPE + np.arange(MLA_NOPE); k_sgn[b:b + MLA_NOPE] = 1.0
        lo = b + (HEAD if h % 2 else 0)
        v_src[lo:lo + HEAD] = h * HEAD + np.arange(HEAD); v_sgn[lo:lo + HEAD] = 1.0
    return (q_src, q_sgn), (r_src, r_sgn), (k_src, k_sgn), (v_src, v_sgn)


def _relayout_cols(w, src_sgn):
    src, sgn = src_sgn
    n = len(src)
    pieces, a = [], 0
    while a < n:
        b = a + 1
        while b < n and sgn[b] == sgn[a] and (sgn[a] == 0 or src[b] == src[b - 1] + 1):
            b += 1
        if sgn[a] == 0:
            pieces.append(jnp.zeros((w.shape[0], b - a), w.dtype))
        else:
            run = w[:, int(src[a]):int(src[a]) + (b - a)]
            pieces.append(run if sgn[a] > 0 else -run)
        a = b
    return jnp.concatenate(pieces, axis=1)


def _layer_norm(v, g, b):
    mu = jnp.mean(v, axis=-1, keepdims=True)
    c = v - mu
    var = jnp.mean(c * c, axis=-1, keepdims=True)
    return c * lax.rsqrt(var + NORM_EPS) * g + b


def _cparams(*sem):
    return pltpu.CompilerParams(dimension_semantics=sem, vmem_limit_bytes=VMEM_LIMIT)


def _rope_kernel(pos_ref, inv_ref, cos_ref, sin_ref):
    ang = pos_ref[...].astype(f32) * inv_ref[...]
    cos_ref[...] = jnp.cos(ang)
    sin_ref[...] = jnp.sin(ang)


def rope_tables(positions, tm=1024):
    t = positions.size
    half = MLA_ROPE // 2
    inv = ROPE_THETA ** (-jnp.arange(half, dtype=f32) / half)
    inv_row = jnp.zeros((1, LANES), f32)
    inv_row = inv_row.at[0, ROPE_LANE0:ROPE_LANE0 + half].set(inv)
    inv_row = inv_row.at[0, ROPE_LANE0 + half:ROPE_LANE0 + 2 * half].set(inv)
    tm = min(tm, t)
    return pl.pallas_call(
        _rope_kernel,
        grid=(t // tm,),
        in_specs=[pl.BlockSpec((tm, 1), lambda i: (i, 0)), pl.BlockSpec((1, LANES), lambda i: (0, 0))],
        out_specs=[pl.BlockSpec((tm, LANES), lambda i: (i, 0))] * 2,
        out_shape=[jax.ShapeDtypeStruct((t, LANES), f32)] * 2,
        compiler_params=_cparams("arbitrary"),
        name="rope_tables",
    )(positions.reshape(t, 1), inv_row)


def _inproj_kernel(x_ref, w_ref, sc_ref, cf_ref, mc_ref, kr_ref, q_ref, k_ref, v_ref):
    xb = x_ref[...].astype(bf16)
    outs = (sc_ref, cf_ref, mc_ref, kr_ref, q_ref, k_ref, v_ref)
    for j, o_ref in enumerate(outs):
        acc = jnp.dot(xb, w_ref[:, SEG_OFF[j]:SEG_OFF[j + 1]], preferred_element_type=f32)
        o_ref[...] = acc.astype(o_ref.dtype)


def inproj(x2d, w_cat, tm=512):
    t = x2d.shape[0]
    tm = min(tm, t)
    dts = (f32, f32, f32, f32, bf16, bf16, bf16)
    return pl.pallas_call(
        _inproj_kernel,
        grid=(t // tm,),
        in_specs=[pl.BlockSpec((tm, D_MODEL), lambda i: (i, 0)),
                  pl.BlockSpec((D_MODEL, IN_COLS_PADDED), lambda i: (0, 0))],
        out_specs=[pl.BlockSpec((tm, w), lambda i: (i, 0)) for w in SEGS],
        out_shape=[jax.ShapeDtypeStruct((t, w), d) for w, d in zip(SEGS, dts)],
        compiler_params=_cparams("arbitrary"),
        name="inproj",
    )(x2d, w_cat)


CONV_ROWS = 64
SC_HALO = 8
CF_HALO = 32


def _local_mix_kernel(sc_ref, sch_ref, cf_ref, cfh_ref, scw_ref, cfw_ref, cfb_ref, lng_ref, lnb_ref,
                      ysc_ref, ycf_ref, pad_sc, pad_cf):
    ts = sc_ref.shape[0]
    first = pl.program_id(1) == 0
    g = GROUP

    pad_sc[SC_HALO:SC_HALO + ts, :] = sc_ref[:, g:2 * g] * sc_ref[:, 2 * g:3 * g]
    halo = sch_ref[:, g:2 * g] * sch_ref[:, 2 * g:3 * g]
    pad_sc[0:SC_HALO, :] = jnp.where(first, 0.0, halo)
    for c in range(ts // CONV_ROWS):
        r0 = c * CONV_ROWS
        acc = jnp.zeros((CONV_ROWS, g), f32)
        for k in range(SC_KERNEL):
            o = SC_HALO + r0 - (SC_KERNEL - 1) + k
            acc = acc + scw_ref[k:k + 1, :] * pad_sc[o:o + CONV_ROWS, :]
        ysc_ref[r0:r0 + CONV_ROWS, :] = sc_ref[r0:r0 + CONV_ROWS, 0:g] * acc

    pad_cf[CF_HALO:CF_HALO + ts, :] = cf_ref[:, 0:g] * jax.nn.sigmoid(cf_ref[:, g:2 * g])
    halo = cfh_ref[:, 0:g] * jax.nn.sigmoid(cfh_ref[:, g:2 * g])
    pad_cf[0:CF_HALO, :] = jnp.where(first, 0.0, halo)
    for c in range(ts // CONV_ROWS):
        r0 = c * CONV_ROWS
        acc = jnp.zeros((CONV_ROWS, g), f32) + cfb_ref[...]
        for k in range(CF_KERNEL):
            o = CF_HALO + r0 - (CF_KERNEL - 1) + k
            acc = acc + cfw_ref[k:k + 1, :] * pad_cf[o:o + CONV_ROWS, :]
        u = _layer_norm(acc, lng_ref[...], lnb_ref[...])
        ycf_ref[r0:r0 + CONV_ROWS, :] = u * jax.nn.sigmoid(u)


def local_mix(sc, cf, sc_w, cf_w, cf_b, ln_g, ln_b, batch, seq, ts=512):
    t = sc.shape[0]
    ts = min(ts, seq)
    nsb = seq // ts
    main = lambda b, n: (b * nsb + n, 0)

    def halo(rows):
        per = ts // rows
        return lambda b, n: (jnp.maximum((b * nsb + n) * per - 1, 0), 0)

    row = lambda w: pl.BlockSpec((1, w), lambda b, n: (0, 0))
    return pl.pallas_call(
        _local_mix_kernel,
        grid=(batch, nsb),
        in_specs=[pl.BlockSpec((ts, SEG_SC), main), pl.BlockSpec((SC_HALO, SEG_SC), halo(SC_HALO)),
                  pl.BlockSpec((ts, SEG_CF), main), pl.BlockSpec((CF_HALO, SEG_CF), halo(CF_HALO)),
                  pl.BlockSpec((SC_KERNEL, GROUP), lambda b, n: (0, 0)),
                  pl.BlockSpec((CF_KERNEL, GROUP), lambda b, n: (0, 0)),
                  row(GROUP), row(GROUP), row(GROUP)],
        out_specs=[pl.BlockSpec((ts, GROUP), main)] * 2,
        out_shape=[jax.ShapeDtypeStruct((t, GROUP), f32)] * 2,
        scratch_shapes=[pltpu.VMEM((SC_HALO + ts, GROUP), f32), pltpu.VMEM((CF_HALO + ts, GROUP), f32)],
        compiler_params=_cparams("arbitrary", "arbitrary"),
        name="local_mix",
    )(sc, sc, cf, cf, sc_w, cf_w, cf_b.reshape(1, GROUP), ln_g.reshape(1, GROUP), ln_b.reshape(1, GROUP))


def _swa_kernel(sinks_ref, q_ref, kc_ref, kp_ref, vc_ref, vp_ref, o_ref):
    n = pl.program_id(1)
    w = WINDOW
    kk = jnp.concatenate([kp_ref[...], kc_ref[...]], axis=0)
    vv = jnp.concatenate([vp_ref[...], vc_ref[...]], axis=0)
    qi = lax.broadcasted_iota(jnp.int32, (w, 2 * w), 0)
    kj = lax.broadcasted_iota(jnp.int32, (w, 2 * w), 1)
    dist = qi + w - kj
    valid = (dist >= 0) & (dist < w) & ((n > 0) | (kj >= w))
    distf = dist.astype(f32)
    group = SWA_Q_HEADS // SWA_KV_HEADS
    for g in range(SWA_KV_HEADS):
        acc = jnp.zeros((w, LANES), f32)
        for gi in range(group):
            hq = g * group + gi
            slope = 2.0 ** (-8.0 * (hq + 1) / SWA_Q_HEADS)
            s = lax.dot_general(q_ref[:, hq * LANES:(hq + 1) * LANES], kk[:, g * LANES:(g + 1) * LANES],
                                (((1,), (1,)), ((), ())), preferred_element_type=f32)
            s = s * (HEAD ** -0.5) - slope * distf
            s = jnp.where(valid, s, -jnp.inf)
            sink = sinks_ref[hq]
            m = jnp.maximum(jnp.max(s, axis=-1, keepdims=True), sink)
            e = jnp.exp(s - m)
            den = jnp.sum(e, axis=-1, keepdims=True) + jnp.exp(sink - m)
            p = (e / den).astype(bf16)
            blk = 2 * g + gi
            acc = acc + jnp.dot(p, vv[:, blk * LANES:(blk + 1) * LANES], preferred_element_type=f32)
        o_ref[:, g * LANES:(g + 1) * LANES] = acc


def swa(q, k, v, sinks, batch, seq):
    t = q.shape[0]
    nb = seq // WINDOW
    cur = lambda b, n: (b * nb + n, 0)
    prev = lambda b, n: (b * nb + jnp.maximum(n - 1, 0), 0)
    return pl.pallas_call(
        _swa_kernel,
        grid=(batch, nb),
        in_specs=[pl.BlockSpec(memory_space=pltpu.SMEM),
                  pl.BlockSpec((WINDOW, SEG_SWQ), cur),
                  pl.BlockSpec((WINDOW, SEG_SWK), cur), pl.BlockSpec((WINDOW, SEG_SWK), prev),
                  pl.BlockSpec((WINDOW, SEG_SWV), cur), pl.BlockSpec((WINDOW, SEG_SWV), prev)],
        out_specs=pl.BlockSpec((WINDOW, GROUP), cur),
        out_shape=jax.ShapeDtypeStruct((t, GROUP), f32),
        compiler_params=_cparams("arbitrary", "arbitrary"),
        name="swa",
    )(sinks, q, k, k, v, v)


def _mla_prep_kernel(mc_ref, kr_ref, cos_ref, sin_ref, qg_ref, kg_ref, wq_ref, wqr_ref, wk_ref, wv_ref,
                     q_ref, k_ref, v_ref):
    cq = mc_ref[:, 0:MLA_Q_RANK]
    ckv = mc_ref[:, MLA_Q_RANK:MLA_Q_RANK + MLA_KV_RANK]
    cqn = (cq * lax.rsqrt(jnp.mean(cq * cq, axis=-1, keepdims=True) + NORM_EPS) * qg_ref[...]).astype(bf16)
    ckvn = (ckv * lax.rsqrt(jnp.mean(ckv * ckv, axis=-1, keepdims=True) + NORM_EPS) * kg_ref[...]).astype(bf16)
    cos = cos_ref[...]
    sin = sin_ref[...]
    k_rope = kr_ref[:, 0:LANES] * cos + kr_ref[:, LANES:2 * LANES] * sin
    v_ref[...] = jnp.dot(ckvn, wv_ref[...], preferred_element_type=f32).astype(bf16)
    for h in range(MLA_HEADS):
        sl = slice(h * LANES, (h + 1) * LANES)
        qh = jnp.dot(cqn, wq_ref[:, sl], preferred_element_type=f32)
        qr = jnp.dot(cqn, wqr_ref[:, sl], preferred_element_type=f32)
        q_ref[:, sl] = (qh * cos + qr * sin).astype(bf16)
        kh = jnp.dot(ckvn, wk_ref[:, sl], preferred_element_type=f32)
        k_ref[:, sl] = (kh + k_rope).astype(bf16)


def mla_prep(mc, kr, cos_t, sin_t, q_gain, kv_gain, wq, wqr, wk, wv, tm=512):
    t = mc.shape[0]
    tm = min(tm, t)
    n = MLA_HEADS * LANES
    tok = lambda w: pl.BlockSpec((tm, w), lambda i: (i, 0))
    full = lambda r, c: pl.BlockSpec((r, c), lambda i: (0, 0))
    return pl.pallas_call(
        _mla_prep_kernel,
        grid=(t // tm,),
        in_specs=[tok(SEG_MC), tok(SEG_KR), tok(LANES), tok(LANES),
                  full(1, MLA_Q_RANK), full(1, MLA_KV_RANK),
                  full(MLA_Q_RANK, n), full(MLA_Q_RANK, n), full(MLA_KV_RANK, n), full(MLA_KV_RANK, n)],
        out_specs=[tok(n)] * 3,
        out_shape=[jax.ShapeDtypeStruct((t, n), bf16)] * 3,
        compiler_params=_cparams("arbitrary"),
        name="mla_prep",
    )(mc, kr, cos_t, sin_t, q_gain.reshape(1, -1), kv_gain.reshape(1, -1), wq, wqr, wk, wv)


def _mla_attn_kernel(q_ref, k_ref, v_ref, o_ref):
    n = pl.program_id(1)
    qb = q_ref.shape[0]
    scale = (MLA_NOPE + MLA_ROPE) ** -0.5
    q_idx = n * qb + lax.broadcasted_iota(jnp.int32, (qb, qb), 0)
    lane = lax.broadcasted_iota(jnp.int32, (qb, LANES), 1)
    low = lane < HEAD
    npair = MLA_HEADS // 2

    def chunk(j, carry):
        ms, ls, accs = carry
        r0 = pl.multiple_of(j * qb, qb)
        k_idx = j * qb + lax.broadcasted_iota(jnp.int32, (qb, qb), 1)
        causal = k_idx <= q_idx
        new_ms, new_ls, new_accs = [], [], []
        for pr in range(npair):
            acc = accs[pr]
            alphas, pvs = [], []
            for hh in range(2):
                h = 2 * pr + hh
                sl = slice(h * LANES, (h + 1) * LANES)
                s = lax.dot_general(q_ref[:, sl], k_ref[pl.ds(r0, qb), sl],
                                    (((1,), (1,)), ((), ())), preferred_element_type=f32) * scale
                s = jnp.where(causal, s, -jnp.inf)
                m_new = jnp.maximum(ms[h], jnp.max(s, axis=-1, keepdims=True))
                alpha = jnp.exp(ms[h] - m_new)
                e = jnp.exp(s - m_new)
                new_ls.append(alpha * ls[h] + jnp.sum(e, axis=-1, keepdims=True))
                new_ms.append(m_new)
                alphas.append(alpha)
                pvs.append(jnp.dot(e.astype(bf16), v_ref[pl.ds(r0, qb), sl], preferred_element_type=f32))
            acc = acc * jnp.where(low, alphas[0], alphas[1]) + pvs[0] + pvs[1]
            new_accs.append(acc)
        return tuple(new_ms), tuple(new_ls), tuple(new_accs)

    init = (tuple(jnp.full((qb, 1), -jnp.inf, f32) for _ in range(MLA_HEADS)),
            tuple(jnp.zeros((qb, 1), f32) for _ in range(MLA_HEADS)),
            tuple(jnp.zeros((qb, LANES), f32) for _ in range(npair)))
    ms, ls, accs = lax.fori_loop(0, n + 1, chunk, init)
    for pr in range(npair):
        den = jnp.where(low, ls[2 * pr], ls[2 * pr + 1])
        o_ref[:, pr * LANES:(pr + 1) * LANES] = accs[pr] / den


def mla_attn(q, k, v, batch, seq, qb=512):
    t = q.shape[0]
    qb = min(qb, seq)
    nq = seq // qb
    n = MLA_HEADS * LANES
    return pl.pallas_call(
        _mla_attn_kernel,
        grid=(batch, nq),
        in_specs=[pl.BlockSpec((qb, n), lambda b, i: (b * nq + i, 0)),
                  pl.BlockSpec((seq, n), lambda b, i: (b, 0)),
                  pl.BlockSpec((seq, n), lambda b, i: (b, 0))],
        out_specs=pl.BlockSpec((qb, GROUP), lambda b, i: (b * nq + i, 0)),
        out_shape=jax.ShapeDtypeStruct((t, GROUP), f32),
        compiler_params=_cparams("arbitrary", "arbitrary"),
        name="mla_attn",
    )(q, k, v)


def _outproj_kernel(ysc_ref, ymla_ref, ycf_ref, ysw_ref, x_ref, mg_ref, w_ref, g_ref, b_ref, h_ref):
    tm = x_ref.shape[0]
    low = lax.broadcasted_iota(jnp.int32, (tm, LANES), 1) < HEAD
    acc = ALPHA * x_ref[...]
    for j, y_ref in enumerate((ysc_ref, ymla_ref, ycf_ref, ysw_ref)):
        for c in range(GROUP // LANES):
            col = j * GROUP + c * LANES
            y = y_ref[:, c * LANES:(c + 1) * LANES]
            y2 = y * y
            s_lo = jnp.sum(jnp.where(low, y2, 0.0), axis=-1, keepdims=True)
            s_hi = jnp.sum(jnp.where(low, 0.0, y2), axis=-1, keepdims=True)
            ms = jnp.where(low, s_lo, s_hi) * (1.0 / HEAD)
            yn = (y * lax.rsqrt(ms + NORM_EPS) * mg_ref[:, col:col + LANES]).astype(bf16)
            acc = acc + jnp.dot(yn, w_ref[col:col + LANES, :], preferred_element_type=f32)
    h_ref[...] = _layer_norm(acc, g_ref[...], b_ref[...])


def outproj(ysc, ymla, ycf, ysw, x2d, mix_g, w_out, ln_g, ln_b, tm=512):
    t = x2d.shape[0]
    tm = min(tm, t)
    tok = lambda w: pl.BlockSpec((tm, w), lambda i: (i, 0))
    full = lambda r, c: pl.BlockSpec((r, c), lambda i: (0, 0))
    return pl.pallas_call(
        _outproj_kernel,
        grid=(t // tm,),
        in_specs=[tok(GROUP)] * 4 + [tok(D_MODEL), full(1, D_MODEL), full(D_MODEL, D_MODEL),
                                     full(1, D_MODEL), full(1, D_MODEL)],
        out_specs=tok(D_MODEL),
        out_shape=jax.ShapeDtypeStruct((t, D_MODEL), f32),
        compiler_params=_cparams("arbitrary"),
        name="outproj",
    )(ysc, ymla, ycf, ysw, x2d, mix_g.reshape(1, -1), w_out, ln_g.reshape(1, -1), ln_b.reshape(1, -1))


def _topk_rows(s, payload, k):
    rows = s.shape[0]
    rid = lax.broadcasted_iota(jnp.int32, s.shape, 0)
    vals, pays = [], []
    for _ in range(k):
        m = jnp.max(s, axis=0, keepdims=True)
        first = jnp.min(jnp.where(s == m, rid, rows), axis=0, keepdims=True)
        hit = rid == first
        vals.append(m)
        pays.append(first if payload is None else jnp.sum(jnp.where(hit, payload, 0), axis=0, keepdims=True))
        s = jnp.where(hit, -jnp.inf, s)
    return jnp.concatenate(vals, axis=0), jnp.concatenate(pays, axis=0)


def _pair_candidates(a, b, combine):
    tm = a.shape[1]
    low4 = lax.broadcasted_iota(jnp.int32, (SUBLANES, tm), 0) < 4
    row = lambda v, i: jnp.broadcast_to(v[i:i + 1, :], (SUBLANES, tm))
    two = lambda v, i: jnp.where(low4, row(v, i), row(v, i + 1))
    b8 = b[0:SUBLANES]
    b44 = jnp.where(low4, b8, pltpu.roll(b8, 4, axis=0))
    firsts = [row(a, 0), row(a, 0), row(a, 1), row(a, 2), row(a, 3), two(a, 4), two(a, 6), a[SUBLANES:2 * SUBLANES]]
    seconds = [b8, b[SUBLANES:2 * SUBLANES], b8, b8, b8, b44, b44, row(b, 0)]
    return jnp.concatenate([combine(x, y) for x, y in zip(firsts, seconds)], axis=0)


def _peer_route_kernel(h_ref, wq_ref, keys_ref, idx_ref, gate_ref):
    k = PEER_TOPK
    q = jnp.dot(h_ref[...].astype(bf16), wq_ref[...], preferred_element_type=f32).astype(bf16)
    for h in range(PEER_HEADS):
        tops = []
        for p in range(2):
            hp = 2 * h + p
            st = lax.dot_general(keys_ref[hp], q[:, hp * PEER_HALF:(hp + 1) * PEER_HALF],
                                 (((1,), (1,)), ((), ())), preferred_element_type=f32)
            tops.append(_topk_rows(st, None, k))
        (a, ia), (b, ib) = tops
        cand_s = _pair_candidates(a, b, lambda x, y: x + y)
        cand_i = _pair_candidates(ia, ib, lambda x, y: x * PEER_NKEYS + y)
        best_s, best_i = _topk_rows(cand_s, cand_i, k)
        e = jnp.exp(best_s - jnp.max(best_s, axis=0, keepdims=True))
        gate_ref[h * k:(h + 1) * k, :] = e / jnp.sum(e, axis=0, keepdims=True)
        idx_ref[h * k:(h + 1) * k, :] = best_i


def peer_route(h2d, wq, keys, tm=256):
    t = h2d.shape[0]
    tm = min(tm, t)
    nq = wq.shape[1]
    return pl.pallas_call(
        _peer_route_kernel,
        grid=(t // tm,),
        in_specs=[pl.BlockSpec((tm, D_MODEL), lambda i: (i, 0)),
                  pl.BlockSpec((D_MODEL, nq), lambda i: (0, 0)),
                  pl.BlockSpec(keys.shape, lambda i: (0, 0, 0))],
        out_specs=[pl.BlockSpec((PEER_SLOTS, tm), lambda i: (0, i))] * 2,
        out_shape=[jax.ShapeDtypeStruct((PEER_SLOTS, t), jnp.int32), jax.ShapeDtypeStruct((PEER_SLOTS, t), f32)],
        compiler_params=_cparams("arbitrary"),
        name="peer_route",
    )(h2d, wq, keys)


PEER_TILE = 128
PEER_NBUF = 16
PEER_GROUP = 4
D_CHUNKS = D_MODEL // LANES
SLAB_ROWS = 2 * D_CHUNKS
V_ACCS = 4


def _slab_layer_norm(z, g, b):
    def mean(v):
        return jnp.sum(jnp.sum(v, axis=2, keepdims=True), axis=1, keepdims=True) * (1.0 / D_MODEL)
    c = z - mean(z)
    return c * lax.rsqrt(mean(c * c) + NORM_EPS) * g + b


def _peer_apply_kernel(idx_hbm, gate_ref, h_ref, tab_hbm, g_ref, b_ref, o_ref,
                       idx_smem, gbuf, ybuf, wrep, isem, gsem):
    tt = h_ref.shape[0]
    nb, grp = PEER_NBUF, PEER_GROUP
    ahead = nb - grp
    step = pl.program_id(0)
    last = pl.num_programs(0) - 1
    cur = step % 2
    nxt_step = jnp.minimum(step + 1, last)

    def idx_copy(src_step, half):
        return pltpu.make_async_copy(idx_hbm.at[src_step], idx_smem.at[half], isem.at[half])

    def issue(half, t, buf, s):
        pltpu.make_async_copy(tab_hbm.at[idx_smem[half, t, s]], gbuf.at[buf, pl.ds(s * SLAB_ROWS, SLAB_ROWS), :],
                              gsem.at[buf]).start(priority=s % 2)

    def wait(buf):
        pltpu.make_async_copy(gbuf.at[buf], gbuf.at[buf], gsem.at[buf]).wait()

    @pl.when(step == 0)
    def _():
        first = idx_copy(0, 0)
        first.start()
        first.wait()
        for t0 in range(ahead):
            for s in range(PEER_SLOTS):
                issue(0, t0, t0, s)

    idx_copy(nxt_step, 1 - cur).start()
    lane = lax.broadcasted_iota(jnp.int32, (SUBLANES, LANES), 1)

    def v_step(buf, s, accs):
        v = gbuf[buf, s * SLAB_ROWS + D_CHUNKS:(s + 1) * SLAB_ROWS, :]
        accs[s % V_ACCS] = accs[s % V_ACCS] + v * wrep[buf % (2 * grp), s:s + 1, :]

    def u_step(buf, s, xt, dots):
        r = jnp.sum(gbuf[buf, s * SLAB_ROWS:s * SLAB_ROWS + D_CHUNKS, :] * xt, axis=-1, keepdims=True)
        return jnp.where(lane == s, r, dots)

    def weights(t, dots):
        sc = jnp.sum(dots, axis=0, keepdims=True)
        act = 0.5 * sc * (1.0 + lax.erf(sc * (2.0 ** -0.5)))
        w = gate_ref[pl.ds(t, 1), :] * act
        return jnp.broadcast_to(w, (PEER_SLOTS, LANES)).T

    def combine(buf):
        accs = [jnp.zeros((SUBLANES, LANES), f32) for _ in range(V_ACCS)]
        for s in range(PEER_SLOTS):
            v_step(buf, s, accs)
        return functools.reduce(lambda p, q: p + q, accs)

    def tokens(t0, b0, src_half, src_t0, lagging=True):
        p0 = (b0 + nb - grp) % nb
        for k in range(grp):
            wait(b0 + k)
        xts = [h_ref[t0 + k] for k in range(grp)]
        dots = [jnp.zeros((SUBLANES, LANES), f32) for _ in range(grp)]
        accs = [[jnp.zeros((SUBLANES, LANES), f32) for _ in range(V_ACCS)] for _ in range(grp)]
        for s in range(PEER_SLOTS):
            for k in range(grp):
                if lagging:
                    v_step(p0 + k, s, accs[k])
                dots[k] = u_step(b0 + k, s, xts[k], dots[k])
            for k in range(grp):
                issue(src_half, src_t0 + k, p0 + k, s)
        for k in range(grp):
            wrep[(b0 + k) % (2 * grp)] = weights(t0 + k, dots[k])
        if lagging:
            for k in range(grp):
                ybuf[t0 - grp + k] = functools.reduce(lambda p, q: p + q, accs[k])

    per_body = nb // grp
    nbodies = tt // nb

    def body(t0, source_of, lag_first=True):
        for j in range(per_body):
            half, src = source_of(j)
            tokens(t0 + j * grp, j * grp, half, src, lagging=lag_first or j > 0)

    body(0, lambda j: (cur, j * grp + ahead), lag_first=False)

    def rolled(bi, carry):
        body(bi * nb, lambda j: (cur, bi * nb + j * grp + ahead))
        return carry

    lax.fori_loop(1, nbodies - 1, rolled, 0)
    idx_copy(nxt_step, 1 - cur).wait()
    body(tt - nb, lambda j: (cur, tt - grp) if j == 0 else (1 - cur, (j - 1) * grp))
    for k in range(grp):
        ybuf[tt - grp + k] = combine(nb - grp + k)

    @pl.when(step == last)
    def _():
        for b in range(ahead):
            wait(b)

    o_ref[...] = _slab_layer_norm(ALPHA * h_ref[...] + ybuf[...], g_ref[...], b_ref[...])


def peer_apply(idx_t, gate_t, h2d, table, ln_g, ln_b):
    t = h2d.shape[0]
    tt = min(PEER_TILE, t)
    assert tt % PEER_NBUF == 0 and tt >= 3 * PEER_NBUF and PEER_NBUF % (2 * PEER_GROUP) == 0
    idx = idx_t.T.reshape(t // tt, tt, PEER_SLOTS)
    slab = pl.BlockSpec((D_CHUNKS, LANES), lambda i: (0, 0))
    tok = pl.BlockSpec((tt, D_CHUNKS, LANES), lambda i: (i, 0, 0))
    out = pl.pallas_call(
        _peer_apply_kernel,
        grid=(t // tt,),
        in_specs=[pl.BlockSpec(memory_space=pl.ANY),
                  pl.BlockSpec((tt, PEER_SLOTS), lambda i: (i, 0)),
                  tok,
                  pl.BlockSpec(memory_space=pl.ANY), slab, slab],
        out_specs=tok,
        out_shape=jax.ShapeDtypeStruct((t, D_CHUNKS, LANES), f32),
        scratch_shapes=[pltpu.SMEM((2, tt, PEER_SLOTS), jnp.int32),
                        pltpu.VMEM((PEER_NBUF, PEER_SLOTS * SLAB_ROWS, LANES), f32),
                        pltpu.VMEM((tt, D_CHUNKS, LANES), f32),
                        pltpu.VMEM((2 * PEER_GROUP, PEER_SLOTS, LANES), f32),
                        pltpu.SemaphoreType.DMA((2,)),
                        pltpu.SemaphoreType.DMA((PEER_NBUF,))],
        compiler_params=_cparams("arbitrary"),
        name="peer_apply",
    )(idx, gate_t.T, h2d.reshape(t, D_CHUNKS, LANES), table,
      ln_g.reshape(D_CHUNKS, LANES), ln_b.reshape(D_CHUNKS, LANES))
    return out.reshape(t, D_MODEL)


def kernel(x, positions, w_in, sc_conv_w, mla_q_norm, mla_kv_norm, mla_w_uq, mla_w_uk, mla_w_uv,
           cf_dw_w, cf_dw_b, cf_ln_g, cf_ln_b, swa_sinks, mix_norm_g, w_out, ln1_g, ln1_b,
           peer_w_q, peer_sub_keys, peer_u, peer_v, ln2_g, ln2_b):
    batch, seq, d = x.shape
    t = batch * seq
    depth = w_in.shape[0]
    in_map = _inproj_column_map()
    q_map, qr_map, k_map, v_map = _mla_weight_maps()
    cos_t, sin_t = rope_tables(positions)
    xc = x.reshape(t, d)
    for l in range(depth):
        w_cat = _relayout_cols(w_in[l], in_map).astype(bf16)
        sc, cf, mc, kr, swq, swk, swv = inproj(xc, w_cat)
        y_sc, y_cf = local_mix(sc, cf, sc_conv_w[l], cf_dw_w[l], cf_dw_b[l], cf_ln_g[l], cf_ln_b[l], batch, seq)
        y_sw = swa(swq, swk, swv, swa_sinks[l], batch, seq)
        mq, mk, mv = mla_prep(mc, kr, cos_t, sin_t, mla_q_norm[l], mla_kv_norm[l],
                              _relayout_cols(mla_w_uq[l], q_map).astype(bf16),
                              _relayout_cols(mla_w_uq[l], qr_map).astype(bf16),
                              _relayout_cols(mla_w_uk[l], k_map).astype(bf16),
                              _relayout_cols(mla_w_uv[l], v_map).astype(bf16))
        y_mla = mla_attn(mq, mk, mv, batch, seq)
        h = outproj(y_sc, y_mla, y_cf, y_sw, xc, mix_norm_g[l], w_out[l].astype(bf16), ln1_g[l], ln1_b[l])
        keys = peer_sub_keys[l].reshape(2 * PEER_HEADS, PEER_NKEYS, PEER_HALF).astype(bf16)
        idx_t, gate_t = peer_route(h, peer_w_q[l].astype(bf16), keys)
        experts = peer_u.shape[1]
        table = jnp.concatenate([peer_u[l].reshape(experts, D_CHUNKS, LANES),
                                 peer_v[l].reshape(experts, D_CHUNKS, LANES)], axis=1)
        xc = peer_apply(idx_t, gate_t, h, table, ln2_g[l], ln2_b[l])
    return xc.reshape(batch, seq, d)
```

```python
import functools
import math

import numpy as np
import jax
import jax.numpy as jnp
from jax import lax
from jax.experimental import pallas as pl
from jax.experimental.pallas import tpu as pltpu

f32 = jnp.float32
bf16 = jnp.bfloat16

D_MODEL = 1024
DEPTH = 4
GROUP = 256
HEAD = 64
SC_KERNEL = 3
MLA_HEADS = 4
MLA_NOPE = 64
MLA_ROPE = 32
MLA_Q_RANK = 256
MLA_KV_RANK = 128
ROPE_THETA = 10000.0
Q_BLOCK = 128
CF_KERNEL = 31
SWA_Q_HEADS = 4
SWA_KV_HEADS = 2
WINDOW = 128
PEER_HEADS = 8
PEER_NKEYS = 128
PEER_TOPK = 16
PEER_HALF = 128
PEER_SLOTS = PEER_HEADS * PEER_TOPK
ALPHA = (2 * DEPTH) ** 0.25
NORM_EPS = 1e-5

LANES = 128
SUBLANES = 8
VMEM_LIMIT = 48 * 1024 * 1024

SEG_SC = 3 * GROUP
SEG_CF = 2 * GROUP
SEG_MC = MLA_Q_RANK + MLA_KV_RANK
SEG_KR = 2 * LANES
SEG_SWQ = SWA_Q_HEADS * LANES
SEG_SWK = SWA_KV_HEADS * LANES
SEG_SWV = 2 * SWA_KV_HEADS * LANES
SEGS = (SEG_SC, SEG_CF, SEG_MC, SEG_KR, SEG_SWQ, SEG_SWK, SEG_SWV)
SEG_OFF = tuple(int(v) for v in np.cumsum((0,) + SEGS))
IN_COLS_PADDED = SEG_OFF[-1]
ROPE_LANE0 = MLA_NOPE


def _inproj_column_map():
    o_sc, o_cq, o_ckv, o_kr, o_cf = 0, 768, 1024, 1152, 1184
    o_swq, o_swk, o_swv = 1696, 1952, 2080
    src = np.zeros(IN_COLS_PADDED, np.int32)
    sgn = np.zeros(IN_COLS_PADDED, np.float32)

    def put(dst, cols, sign=1.0):
        src[dst:dst + len(cols)] = cols
        sgn[dst:dst + len(cols)] = sign

    half = MLA_ROPE // 2
    put(SEG_OFF[0], o_sc + np.arange(SEG_SC))
    put(SEG_OFF[1], o_cf + np.arange(SEG_CF))
    put(SEG_OFF[2], o_cq + np.arange(SEG_MC))
    x1 = o_kr + np.arange(half)
    x2 = o_kr + half + np.arange(half)
    put(SEG_OFF[3] + ROPE_LANE0, x1)
    put(SEG_OFF[3] + ROPE_LANE0 + half, x2)
    put(SEG_OFF[3] + LANES + ROPE_LANE0, x2, -1.0)
    put(SEG_OFF[3] + LANES + ROPE_LANE0 + half, x1)
    for h in range(SWA_Q_HEADS):
        put(SEG_OFF[4] + h * LANES, o_swq + h * HEAD + np.arange(HEAD))
    for g in range(SWA_KV_HEADS):
        put(SEG_OFF[5] + g * LANES, o_swk + g * HEAD + np.arange(HEAD))
        put(SEG_OFF[6] + (2 * g) * LANES, o_swv + g * HEAD + np.arange(HEAD))
        put(SEG_OFF[6] + (2 * g + 1) * LANES + HEAD, o_swv + g * HEAD + np.arange(HEAD))
    return src, sgn


def _mla_weight_maps():
    half = MLA_ROPE // 2
    qd = MLA_NOPE + MLA_ROPE
    n = MLA_HEADS * LANES
    q_src = np.zeros(n, np.int32); q_sgn = np.zeros(n, np.float32)
    r_src = np.zeros(n, np.int32); r_sgn = np.zeros(n, np.float32)
    k_src = np.zeros(n, np.int32); k_sgn = np.zeros(n, np.float32)
    v_src = np.zeros(n, np.int32); v_sgn = np.zeros(n, np.float32)
    for h in range(MLA_HEADS):
        b = h * LANES
        q_src[b:b + qd] = h * qd + np.arange(qd); q_sgn[b:b + qd] = 1.0
        x1 = h * qd + MLA_NOPE + np.arange(half)
        x2 = x1 + half
        r_src[b + ROPE_LANE0:b + ROPE_LANE0 + half] = x2; r_sgn[b + ROPE_LANE0:b + ROPE_LANE0 + half] = -1.0
        r_src[b + ROPE_LANE0 + half:b + ROPE_LANE0 + 2 * half] = x1; r_sgn[b + ROPE_LANE0 + half:b + ROPE_LANE0 + 2 * half] = 1.0
        k_src[b:b + MLA_NOPE] = h * MLA_NOPE + np.arange(MLA_NOPE); k_sgn[b:b + MLA_NOPE] = 1.0
        lo = b + (HEAD if h % 2 else 0)
        v_src[lo:lo + HEAD] = h * HEAD + np.arange(HEAD); v_sgn[lo:lo + HEAD] = 1.0
    return (q_src, q_sgn), (r_src, r_sgn), (k_src, k_sgn), (v_src, v_sgn)


def _relayout_cols(w, src_sgn):
    src, sgn = src_sgn
    n = len(src)
    pieces, a = [], 0
    while a < n:
        b = a + 1
        while b < n and sgn[b] == sgn[a] and (sgn[a] == 0 or src[b] == src[b - 1] + 1):
            b += 1
        if sgn[a] == 0:
            pieces.append(jnp.zeros((w.shape[0], b - a), w.dtype))
        else:
            run = w[:, int(src[a]):int(src[a]) + (b - a)]
            pieces.append(run if sgn[a] > 0 else -run)
        a = b
    return jnp.concatenate(pieces, axis=1)


def _layer_norm(v, g, b):
    mu = jnp.mean(v, axis=-1, keepdims=True)
    c = v - mu
    var = jnp.mean(c * c, axis=-1, keepdims=True)
    return c * lax.rsqrt(var + NORM_EPS) * g + b


def _cparams(*sem):
    return pltpu.CompilerParams(dimension_semantics=sem, vmem_limit_bytes=VMEM_LIMIT)


def _rope_kernel(pos_ref, inv_ref, cos_ref, sin_ref):
    ang = pos_ref[...].astype(f32) * inv_ref[...]
    cos_ref[...] = jnp.cos(ang)
    sin_ref[...] = jnp.sin(ang)


def rope_tables(positions, tm=1024):
    t = positions.size
    half = MLA_ROPE // 2
    inv = ROPE_THETA ** (-jnp.arange(half, dtype=f32) / half)
    inv_row = jnp.zeros((1, LANES), f32)
    inv_row = inv_row.at[0, ROPE_LANE0:ROPE_LANE0 + half].set(inv)
    inv_row = inv_row.at[0, ROPE_LANE0 + half:ROPE_LANE0 + 2 * half].set(inv)
    tm = min(tm, t)
    return pl.pallas_call(
        _rope_kernel,
        grid=(t // tm,),
        in_specs=[pl.BlockSpec((tm, 1), lambda i: (i, 0)), pl.BlockSpec((1, LANES), lambda i: (0, 0))],
        out_specs=[pl.BlockSpec((tm, LANES), lambda i: (i, 0))] * 2,
        out_shape=[jax.ShapeDtypeStruct((t, LANES), f32)] * 2,
        compiler_params=_cparams("arbitrary"),
        name="rope_tables",
    )(positions.reshape(t, 1), inv_row)


def _inproj_kernel(x_ref, w_ref, sc_ref, cf_ref, mc_ref, kr_ref, q_ref, k_ref, v_ref):
    xb = x_ref[...].astype(bf16)
    outs = (sc_ref, cf_ref, mc_ref, kr_ref, q_ref, k_ref, v_ref)
    for j, o_ref in enumerate(outs):
        acc = jnp.dot(xb, w_ref[:, SEG_OFF[j]:SEG_OFF[j + 1]], preferred_element_type=f32)
        o_ref[...] = acc.astype(o_ref.dtype)


def inproj(x2d, w_cat, tm=512):
    t = x2d.shape[0]
    tm = min(tm, t)
    dts = (f32, f32, f32, f32, bf16, bf16, bf16)
    return pl.pallas_call(
        _inproj_kernel,
        grid=(t // tm,),
        in_specs=[pl.BlockSpec((tm, D_MODEL), lambda i: (i, 0)),
                  pl.BlockSpec((D_MODEL, IN_COLS_PADDED), lambda i: (0, 0))],
        out_specs=[pl.BlockSpec((tm, w), lambda i: (i, 0)) for w in SEGS],
        out_shape=[jax.ShapeDtypeStruct((t, w), d) for w, d in zip(SEGS, dts)],
        compiler_params=_cparams("arbitrary"),
        name="inproj",
    )(x2d, w_cat)


CONV_ROWS = 64
SC_HALO = 8
CF_HALO = 32


def _local_mix_kernel(sc_ref, sch_ref, cf_ref, cfh_ref, scw_ref, cfw_ref, cfb_ref, lng_ref, lnb_ref,
                      ysc_ref, ycf_ref, pad_sc, pad_cf):
    ts = sc_ref.shape[0]
    first = pl.program_id(1) == 0
    g = GROUP

    pad_sc[SC_HALO:SC_HALO + ts, :] = sc_ref[:, g:2 * g] * sc_ref[:, 2 * g:3 * g]
    halo = sch_ref[:, g:2 * g] * sch_ref[:, 2 * g:3 * g]
    pad_sc[0:SC_HALO, :] = jnp.where(first, 0.0, halo)
    for c in range(ts // CONV_ROWS):
        r0 = c * CONV_ROWS
        acc = jnp.zeros((CONV_ROWS, g), f32)
        for k in range(SC_KERNEL):
            o = SC_HALO + r0 - (SC_KERNEL - 1) + k
            acc = acc + scw_ref[k:k + 1, :] * pad_sc[o:o + CONV_ROWS, :]
        ysc_ref[r0:r0 + CONV_ROWS, :] = sc_ref[r0:r0 + CONV_ROWS, 0:g] * acc

    pad_cf[CF_HALO:CF_HALO + ts, :] = cf_ref[:, 0:g] * jax.nn.sigmoid(cf_ref[:, g:2 * g])
    halo = cfh_ref[:, 0:g] * jax.nn.sigmoid(cfh_ref[:, g:2 * g])
    pad_cf[0:CF_HALO, :] = jnp.where(first, 0.0, halo)
    for c in range(ts // CONV_ROWS):
        r0 = c * CONV_ROWS
        acc = jnp.zeros((CONV_ROWS, g), f32) + cfb_ref[...]
        for k in range(CF_KERNEL):
            o = CF_HALO + r0 - (CF_KERNEL - 1) + k
            acc = acc + cfw_ref[k:k + 1, :] * pad_cf[o:o + CONV_ROWS, :]
        u = _layer_norm(acc, lng_ref[...], lnb_ref[...])
        ycf_ref[r0:r0 + CONV_ROWS, :] = u * jax.nn.sigmoid(u)


def local_mix(sc, cf, sc_w, cf_w, cf_b, ln_g, ln_b, batch, seq, ts=512):
    t = sc.shape[0]
    ts = min(ts, seq)
    nsb = seq // ts
    main = lambda b, n: (b * nsb + n, 0)

    def halo(rows):
        per = ts // rows
        return lambda b, n: (jnp.maximum((b * nsb + n) * per - 1, 0), 0)

    row = lambda w: pl.BlockSpec((1, w), lambda b, n: (0, 0))
    return pl.pallas_call(
        _local_mix_kernel,
        grid=(batch, nsb),
        in_specs=[pl.BlockSpec((ts, SEG_SC), main), pl.BlockSpec((SC_HALO, SEG_SC), halo(SC_HALO)),
                  pl.BlockSpec((ts, SEG_CF), main), pl.BlockSpec((CF_HALO, SEG_CF), halo(CF_HALO)),
                  pl.BlockSpec((SC_KERNEL, GROUP), lambda b, n: (0, 0)),
                  pl.BlockSpec((CF_KERNEL, GROUP), lambda b, n: (0, 0)),
                  row(GROUP), row(GROUP), row(GROUP)],
        out_specs=[pl.BlockSpec((ts, GROUP), main)] * 2,
        out_shape=[jax.ShapeDtypeStruct((t, GROUP), f32)] * 2,
        scratch_shapes=[pltpu.VMEM((SC_HALO + ts, GROUP), f32), pltpu.VMEM((CF_HALO + ts, GROUP), f32)],
        compiler_params=_cparams("arbitrary", "arbitrary"),
        name="local_mix",
    )(sc, sc, cf, cf, sc_w, cf_w, cf_b.reshape(1, GROUP), ln_g.reshape(1, GROUP), ln_b.reshape(1, GROUP))


def _swa_kernel(sinks_ref, q_ref, kc_ref, kp_ref, vc_ref, vp_ref, o_ref):
    n = pl.program_id(1)
    w = WINDOW
    kk = jnp.concatenate([kp_ref[...], kc_ref[...]], axis=0)
    vv = jnp.concatenate([vp_ref[...], vc_ref[...]], axis=0)
    qi = lax.broadcasted_iota(jnp.int32, (w, 2 * w), 0)
    kj = lax.broadcasted_iota(jnp.int32, (w, 2 * w), 1)
    dist = qi + w - kj
    valid = (dist >= 0) & (dist < w) & ((n > 0) | (kj >= w))
    distf = dist.astype(f32)
    group = SWA_Q_HEADS // SWA_KV_HEADS
    for g in range(SWA_KV_HEADS):
        acc = jnp.zeros((w, LANES), f32)
        for gi in range(group):
            hq = g * group + gi
            slope = 2.0 ** (-8.0 * (hq + 1) / SWA_Q_HEADS)
            s = lax.dot_general(q_ref[:, hq * LANES:(hq + 1) * LANES], kk[:, g * LANES:(g + 1) * LANES],
                                (((1,), (1,)), ((), ())), preferred_element_type=f32)
            s = s * (HEAD ** -0.5) - slope * distf
            s = jnp.where(valid, s, -jnp.inf)
            sink = sinks_ref[hq]
            m = jnp.maximum(jnp.max(s, axis=-1, keepdims=True), sink)
            e = jnp.exp(s - m)
            den = jnp.sum(e, axis=-1, keepdims=True) + jnp.exp(sink - m)
            p = (e / den).astype(bf16)
            blk = 2 * g + gi
            acc = acc + jnp.dot(p, vv[:, blk * LANES:(blk + 1) * LANES], preferred_element_type=f32)
        o_ref[:, g * LANES:(g + 1) * LANES] = acc


def swa(q, k, v, sinks, batch, seq):
    t = q.shape[0]
    nb = seq // WINDOW
    cur = lambda b, n: (b * nb + n, 0)
    prev = lambda b, n: (b * nb + jnp.maximum(n - 1, 0), 0)
    return pl.pallas_call(
        _swa_kernel,
        grid=(batch, nb),
        in_specs=[pl.BlockSpec(memory_space=pltpu.SMEM),
                  pl.BlockSpec((WINDOW, SEG_SWQ), cur),
                  pl.BlockSpec((WINDOW, SEG_SWK), cur), pl.BlockSpec((WINDOW, SEG_SWK), prev),
                  pl.BlockSpec((WINDOW, SEG_SWV), cur), pl.BlockSpec((WINDOW, SEG_SWV), prev)],
        out_specs=pl.BlockSpec((WINDOW, GROUP), cur),
        out_shape=jax.ShapeDtypeStruct((t, GROUP), f32),
        compiler_params=_cparams("arbitrary", "arbitrary"),
        name="swa",
    )(sinks, q, k, k, v, v)


def _mla_prep_kernel(mc_ref, kr_ref, cos_ref, sin_ref, qg_ref, kg_ref, wq_ref, wqr_ref, wk_ref, wv_ref,
                     q_ref, k_ref, v_ref):
    cq = mc_ref[:, 0:MLA_Q_RANK]
    ckv = mc_ref[:, MLA_Q_RANK:MLA_Q_RANK + MLA_KV_RANK]
    cqn = (cq * lax.rsqrt(jnp.mean(cq * cq, axis=-1, keepdims=True) + NORM_EPS) * qg_ref[...]).astype(bf16)
    ckvn = (ckv * lax.rsqrt(jnp.mean(ckv * ckv, axis=-1, keepdims=True) + NORM_EPS) * kg_ref[...]).astype(bf16)
    cos = cos_ref[...]
    sin = sin_ref[...]
    k_rope = kr_ref[:, 0:LANES] * cos + kr_ref[:, LANES:2 * LANES] * sin
    v_ref[...] = jnp.dot(ckvn, wv_ref[...], preferred_element_type=f32).astype(bf16)
    for h in range(MLA_HEADS):
        sl = slice(h * LANES, (h + 1) * LANES)
        qh = jnp.dot(cqn, wq_ref[:, sl], preferred_element_type=f32)
        qr = jnp.dot(cqn, wqr_ref[:, sl], preferred_element_type=f32)
        q_ref[:, sl] = (qh * cos + qr * sin).astype(bf16)
        kh = jnp.dot(ckvn, wk_ref[:, sl], preferred_element_type=f32)
        k_ref[:, sl] = (kh + k_rope).astype(bf16)


def mla_prep(mc, kr, cos_t, sin_t, q_gain, kv_gain, wq, wqr, wk, wv, tm=512):
    t = mc.shape[0]
    tm = min(tm, t)
    n = MLA_HEADS * LANES
    tok = lambda w: pl.BlockSpec((tm, w), lambda i: (i, 0))
    full = lambda r, c: pl.BlockSpec((r, c), lambda i: (0, 0))
    return pl.pallas_call(
        _mla_prep_kernel,
        grid=(t // tm,),
        in_specs=[tok(SEG_MC), tok(SEG_KR), tok(LANES), tok(LANES),
                  full(1, MLA_Q_RANK), full(1, MLA_KV_RANK),
                  full(MLA_Q_RANK, n), full(MLA_Q_RANK, n), full(MLA_KV_RANK, n), full(MLA_KV_RANK, n)],
        out_specs=[tok(n)] * 3,
        out_shape=[jax.ShapeDtypeStruct((t, n), bf16)] * 3,
        compiler_params=_cparams("arbitrary"),
        name="mla_prep",
    )(mc, kr, cos_t, sin_t, q_gain.reshape(1, -1), kv_gain.reshape(1, -1), wq, wqr, wk, wv)


def _mla_attn_kernel(q_ref, k_ref, v_ref, o_ref):
    n = pl.program_id(1)
    qb = q_ref.shape[0]
    scale = (MLA_NOPE + MLA_ROPE) ** -0.5
    q_idx = n * qb + lax.broadcasted_iota(jnp.int32, (qb, qb), 0)
    lane = lax.broadcasted_iota(jnp.int32, (qb, LANES), 1)
    low = lane < HEAD
    npair = MLA_HEADS // 2

    def chunk(j, carry):
        ms, ls, accs = carry
        r0 = pl.multiple_of(j * qb, qb)
        k_idx = j * qb + lax.broadcasted_iota(jnp.int32, (qb, qb), 1)
        causal = k_idx <= q_idx
        new_ms, new_ls, new_accs = [], [], []
        for pr in range(npair):
            acc = accs[pr]
            alphas, pvs = [], []
            for hh in range(2):
                h = 2 * pr + hh
                sl = slice(h * LANES, (h + 1) * LANES)
                s = lax.dot_general(q_ref[:, sl], k_ref[pl.ds(r0, qb), sl],
                                    (((1,), (1,)), ((), ())), preferred_element_type=f32) * scale
                s = jnp.where(causal, s, -jnp.inf)
                m_new = jnp.maximum(ms[h], jnp.max(s, axis=-1, keepdims=True))
                alpha = jnp.exp(ms[h] - m_new)
                e = jnp.exp(s - m_new)
                new_ls.append(alpha * ls[h] + jnp.sum(e, axis=-1, keepdims=True))
                new_ms.append(m_new)
                alphas.append(alpha)
                pvs.append(jnp.dot(e.astype(bf16), v_ref[pl.ds(r0, qb), sl], preferred_element_type=f32))
            acc = acc * jnp.where(low, alphas[0], alphas[1]) + pvs[0] + pvs[1]
            new_accs.append(acc)
        return tuple(new_ms), tuple(new_ls), tuple(new_accs)

    init = (tuple(jnp.full((qb, 1), -jnp.inf, f32) for _ in range(MLA_HEADS)),
            tuple(jnp.zeros((qb, 1), f32) for _ in range(MLA_HEADS)),
            tuple(jnp.zeros((qb, LANES), f32) for _ in range(npair)))
    ms, ls, accs = lax.fori_loop(0, n + 1, chunk, init)
    for pr in range(npair):
        den = jnp.where(low, ls[2 * pr], ls[2 * pr + 1])
        o_ref[:, pr * LANES:(pr + 1) * LANES] = accs[pr] / den


def mla_attn(q, k, v, batch, seq, qb=512):
    t = q.shape[0]
    qb = min(qb, seq)
    nq = seq // qb
    n = MLA_HEADS * LANES
    return pl.pallas_call(
        _mla_attn_kernel,
        grid=(batch, nq),
        in_specs=[pl.BlockSpec((qb, n), lambda b, i: (b * nq + i, 0)),
                  pl.BlockSpec((seq, n), lambda b, i: (b, 0)),
                  pl.BlockSpec((seq, n), lambda b, i: (b, 0))],
        out_specs=pl.BlockSpec((qb, GROUP), lambda b, i: (b * nq + i, 0)),
        out_shape=jax.ShapeDtypeStruct((t, GROUP), f32),
        compiler_params=_cparams("arbitrary", "arbitrary"),
        name="mla_attn",
    )(q, k, v)


def _outproj_kernel(ysc_ref, ymla_ref, ycf_ref, ysw_ref, x_ref, mg_ref, w_ref, g_ref, b_ref, h_ref):
    tm = x_ref.shape[0]
    low = lax.broadcasted_iota(jnp.int32, (tm, LANES), 1) < HEAD
    acc = ALPHA * x_ref[...]
    for j, y_ref in enumerate((ysc_ref, ymla_ref, ycf_ref, ysw_ref)):
        for c in range(GROUP // LANES):
            col = j * GROUP + c * LANES
            y = y_ref[:, c * LANES:(c + 1) * LANES]
            y2 = y * y
            s_lo = jnp.sum(jnp.where(low, y2, 0.0), axis=-1, keepdims=True)
            s_hi = jnp.sum(jnp.where(low, 0.0, y2), axis=-1, keepdims=True)
            ms = jnp.where(low, s_lo, s_hi) * (1.0 / HEAD)
            yn = (y * lax.rsqrt(ms + NORM_EPS) * mg_ref[:, col:col + LANES]).astype(bf16)
            acc = acc + jnp.dot(yn, w_ref[col:col + LANES, :], preferred_element_type=f32)
    h_ref[...] = _layer_norm(acc, g_ref[...], b_ref[...])


def outproj(ysc, ymla, ycf, ysw, x2d, mix_g, w_out, ln_g, ln_b, tm=512):
    t = x2d.shape[0]
    tm = min(tm, t)
    tok = lambda w: pl.BlockSpec((tm, w), lambda i: (i, 0))
    full = lambda r, c: pl.BlockSpec((r, c), lambda i: (0, 0))
    return pl.pallas_call(
        _outproj_kernel,
        grid=(t // tm,),
        in_specs=[tok(GROUP)] * 4 + [tok(D_MODEL), full(1, D_MODEL), full(D_MODEL, D_MODEL),
                                     full(1, D_MODEL), full(1, D_MODEL)],
        out_specs=tok(D_MODEL),
        out_shape=jax.ShapeDtypeStruct((t, D_MODEL), f32),
        compiler_params=_cparams("arbitrary"),
        name="outproj",
    )(ysc, ymla, ycf, ysw, x2d, mix_g.reshape(1, -1), w_out, ln_g.reshape(1, -1), ln_b.reshape(1, -1))


def _topk_rows(s, payload, k):
    rows = s.shape[0]
    rid = lax.broadcasted_iota(jnp.int32, s.shape, 0)
    vals, pays = [], []
    for _ in range(k):
        m = jnp.max(s, axis=0, keepdims=True)
        first = jnp.min(jnp.where(s == m, rid, rows), axis=0, keepdims=True)
        hit = rid == first
        vals.append(m)
        pays.append(first if payload is None else jnp.sum(jnp.where(hit, payload, 0), axis=0, keepdims=True))
        s = jnp.where(hit, -jnp.inf, s)
    return jnp.concatenate(vals, axis=0), jnp.concatenate(pays, axis=0)


def _pair_candidates(a, b, combine):
    tm = a.shape[1]
    low4 = lax.broadcasted_iota(jnp.int32, (SUBLANES, tm), 0) < 4
    row = lambda v, i: jnp.broadcast_to(v[i:i + 1, :], (SUBLANES, tm))
    two = lambda v, i: jnp.where(low4, row(v, i), row(v, i + 1))
    b8 = b[0:SUBLANES]
    b44 = jnp.where(low4, b8, pltpu.roll(b8, 4, axis=0))
    firsts = [row(a, 0), row(a, 0), row(a, 1), row(a, 2), row(a, 3), two(a, 4), two(a, 6), a[SUBLANES:2 * SUBLANES]]
    seconds = [b8, b[SUBLANES:2 * SUBLANES], b8, b8, b8, b44, b44, row(b, 0)]
    return jnp.concatenate([combine(x, y) for x, y in zip(firsts, seconds)], axis=0)


def _peer_route_kernel(h_ref, wq_ref, keys_ref, idx_ref, gate_ref):
    k = PEER_TOPK
    q = jnp.dot(h_ref[...].astype(bf16), wq_ref[...], preferred_element_type=f32).astype(bf16)
    for h in range(PEER_HEADS):
        tops = []
        for p in range(2):
            hp = 2 * h + p
            st = lax.dot_general(keys_ref[hp], q[:, hp * PEER_HALF:(hp + 1) * PEER_HALF],
                                 (((1,), (1,)), ((), ())), preferred_element_type=f32)
            tops.append(_topk_rows(st, None, k))
        (a, ia), (b, ib) = tops
        cand_s = _pair_candidates(a, b, lambda x, y: x + y)
        cand_i = _pair_candidates(ia, ib, lambda x, y: x * PEER_NKEYS + y)
        best_s, best_i = _topk_rows(cand_s, cand_i, k)
        e = jnp.exp(best_s - jnp.max(best_s, axis=0, keepdims=True))
        gate_ref[h * k:(h + 1) * k, :] = e / jnp.sum(e, axis=0, keepdims=True)
        idx_ref[h * k:(h + 1) * k, :] = best_i


def peer_route(h2d, wq, keys, tm=512):
    t = h2d.shape[0]
    tm = min(tm, t)
    nq = wq.shape[1]
    return pl.pallas_call(
        _peer_route_kernel,
        grid=(t // tm,),
        in_specs=[pl.BlockSpec((tm, D_MODEL), lambda i: (i, 0)),
                  pl.BlockSpec((D_MODEL, nq), lambda i: (0, 0)),
                  pl.BlockSpec(keys.shape, lambda i: (0, 0, 0))],
        out_specs=[pl.BlockSpec((PEER_SLOTS, tm), lambda i: (0, i))] * 2,
        out_shape=[jax.ShapeDtypeStruct((PEER_SLOTS, t), jnp.int32), jax.ShapeDtypeStruct((PEER_SLOTS, t), f32)],
        compiler_params=_cparams("arbitrary"),
        name="peer_route",
    )(h2d, wq, keys)


PEER_TILE = 128
PEER_NBUF = 16
PEER_GROUP = 4
D_CHUNKS = D_MODEL // LANES
SLAB_ROWS = 2 * D_CHUNKS
V_ACCS = 4


def _slab_layer_norm(z, g, b):
    def mean(v):
        return jnp.sum(jnp.sum(v, axis=2, keepdims=True), axis=1, keepdims=True) * (1.0 / D_MODEL)
    c = z - mean(z)
    return c * lax.rsqrt(mean(c * c) + NORM_EPS) * g + b


def _peer_table_kernel(u_ref, v_ref, o_ref):
    rows = u_ref.shape[0]
    for c in range(D_CHUNKS):
        sl = slice(c * LANES, (c + 1) * LANES)
        o_ref[pl.ds(c, rows, stride=SLAB_ROWS), :] = u_ref[:, sl]
        o_ref[pl.ds(D_CHUNKS + c, rows, stride=SLAB_ROWS), :] = v_ref[:, sl]


def peer_table(u, v, layer, rows=512):
    experts = u.shape[1]
    out = pl.pallas_call(
        _peer_table_kernel,
        grid=(experts // rows,),
        in_specs=[pl.BlockSpec((None, rows, D_MODEL), lambda i: (layer, i, 0))] * 2,
        out_specs=pl.BlockSpec((rows * SLAB_ROWS, LANES), lambda i: (i, 0)),
        out_shape=jax.ShapeDtypeStruct((experts * SLAB_ROWS, LANES), f32),
        compiler_params=_cparams("arbitrary"),
        name="peer_table",
    )(u, v)
    return out.reshape(experts, SLAB_ROWS, LANES)


def _peer_apply_kernel(idx_hbm, gate_ref, h_ref, tab_hbm, g_ref, b_ref, o_ref,
                       idx_smem, gbuf, ybuf, wrep, isem, gsem):
    tt = h_ref.shape[0]
    nb, grp = PEER_NBUF, PEER_GROUP
    ahead = nb - grp
    step = pl.program_id(0)
    last = pl.num_programs(0) - 1
    cur = step % 2
    nxt_step = jnp.minimum(step + 1, last)

    def idx_copy(src_step, half):
        return pltpu.make_async_copy(idx_hbm.at[src_step], idx_smem.at[half], isem.at[half])

    def issue(half, t, buf, s):
        pltpu.make_async_copy(tab_hbm.at[idx_smem[half, t, s]], gbuf.at[buf, pl.ds(s * SLAB_ROWS, SLAB_ROWS), :],
                              gsem.at[buf]).start(priority=s % 2)

    def wait(buf):
        pltpu.make_async_copy(gbuf.at[buf], gbuf.at[buf], gsem.at[buf]).wait()

    @pl.when(step == 0)
    def _():
        first = idx_copy(0, 0)
        first.start()
        first.wait()
        for t0 in range(ahead):
            for s in range(PEER_SLOTS):
                issue(0, t0, t0, s)

    idx_copy(nxt_step, 1 - cur).start()
    lane = lax.broadcasted_iota(jnp.int32, (SUBLANES, LANES), 1)

    def v_step(buf, s, accs):
        v = gbuf[buf, s * SLAB_ROWS + D_CHUNKS:(s + 1) * SLAB_ROWS, :]
        accs[s % V_ACCS] = accs[s % V_ACCS] + v * wrep[buf % (2 * grp), s:s + 1, :]

    def u_step(buf, s, xt, dots):
        r = jnp.sum(gbuf[buf, s * SLAB_ROWS:s * SLAB_ROWS + D_CHUNKS, :] * xt, axis=-1, keepdims=True)
        return jnp.where(lane == s, r, dots)

    def weights(t, dots):
        sc = jnp.sum(dots, axis=0, keepdims=True)
        act = 0.5 * sc * (1.0 + lax.erf(sc * (2.0 ** -0.5)))
        w = gate_ref[pl.ds(t, 1), :] * act
        return jnp.broadcast_to(w, (PEER_SLOTS, LANES)).T

    def combine(buf):
        accs = [jnp.zeros((SUBLANES, LANES), f32) for _ in range(V_ACCS)]
        for s in range(PEER_SLOTS):
            v_step(buf, s, accs)
        return functools.reduce(lambda p, q: p + q, accs)

    def tokens(t0, b0, src_half, src_t0, lagging=True):
        p0 = (b0 + nb - grp) % nb
        for k in range(grp):
            wait(b0 + k)
        xts = [h_ref[t0 + k] for k in range(grp)]
        dots = [jnp.zeros((SUBLANES, LANES), f32) for _ in range(grp)]
        accs = [[jnp.zeros((SUBLANES, LANES), f32) for _ in range(V_ACCS)] for _ in range(grp)]
        for s in range(PEER_SLOTS):
            for k in range(grp):
                if lagging:
                    v_step(p0 + k, s, accs[k])
                dots[k] = u_step(b0 + k, s, xts[k], dots[k])
            for k in range(grp):
                issue(src_half, src_t0 + k, p0 + k, s)
        for k in range(grp):
            wrep[(b0 + k) % (2 * grp)] = weights(t0 + k, dots[k])
        if lagging:
            for k in range(grp):
                ybuf[t0 - grp + k] = functools.reduce(lambda p, q: p + q, accs[k])

    per_body = nb // grp
    nbodies = tt // nb

    def body(t0, source_of, lag_first=True):
        for j in range(per_body):
            half, src = source_of(j)
            tokens(t0 + j * grp, j * grp, half, src, lagging=lag_first or j > 0)

    body(0, lambda j: (cur, j * grp + ahead), lag_first=False)

    def rolled(bi, carry):
        body(bi * nb, lambda j: (cur, bi * nb + j * grp + ahead))
        return carry

    lax.fori_loop(1, nbodies - 1, rolled, 0)
    idx_copy(nxt_step, 1 - cur).wait()
    body(tt - nb, lambda j: (cur, tt - grp) if j == 0 else (1 - cur, (j - 1) * grp))
    for k in range(grp):
        ybuf[tt - grp + k] = combine(nb - grp + k)

    @pl.when(step == last)
    def _():
        for b in range(ahead):
            wait(b)

    o_ref[...] = _slab_layer_norm(ALPHA * h_ref[...] + ybuf[...], g_ref[...], b_ref[...])


def peer_apply(idx_t, gate_t, h2d, table, ln_g, ln_b):
    t = h2d.shape[0]
    tt = min(PEER_TILE, t)
    assert tt % PEER_NBUF == 0 and tt >= 3 * PEER_NBUF and PEER_NBUF % (2 * PEER_GROUP) == 0
    idx = idx_t.T.reshape(t // tt, tt, PEER_SLOTS)
    slab = pl.BlockSpec((D_CHUNKS, LANES), lambda i: (0, 0))
    tok = pl.BlockSpec((tt, D_CHUNKS, LANES), lambda i: (i, 0, 0))
    out = pl.pallas_call(
        _peer_apply_kernel,
        grid=(t // tt,),
        in_specs=[pl.BlockSpec(memory_space=pl.ANY),
                  pl.BlockSpec((tt, PEER_SLOTS), lambda i: (i, 0)),
                  tok,
                  pl.BlockSpec(memory_space=pl.ANY), slab, slab],
        out_specs=tok,
        out_shape=jax.ShapeDtypeStruct((t, D_CHUNKS, LANES), f32),
        scratch_shapes=[pltpu.SMEM((2, tt, PEER_SLOTS), jnp.int32),
                        pltpu.VMEM((PEER_NBUF, PEER_SLOTS * SLAB_ROWS, LANES), f32),
                        pltpu.VMEM((tt, D_CHUNKS, LANES), f32),
                        pltpu.VMEM((2 * PEER_GROUP, PEER_SLOTS, LANES), f32),
                        pltpu.SemaphoreType.DMA((2,)),
                        pltpu.SemaphoreType.DMA((PEER_NBUF,))],
        compiler_params=_cparams("arbitrary"),
        name="peer_apply",
    )(idx, gate_t.T, h2d.reshape(t, D_CHUNKS, LANES), table,
      ln_g.reshape(D_CHUNKS, LANES), ln_b.reshape(D_CHUNKS, LANES))
    return out.reshape(t, D_MODEL)


def kernel(x, positions, w_in, sc_conv_w, mla_q_norm, mla_kv_norm, mla_w_uq, mla_w_uk, mla_w_uv,
           cf_dw_w, cf_dw_b, cf_ln_g, cf_ln_b, swa_sinks, mix_norm_g, w_out, ln1_g, ln1_b,
           peer_w_q, peer_sub_keys, peer_u, peer_v, ln2_g, ln2_b):
    batch, seq, d = x.shape
    t = batch * seq
    depth = w_in.shape[0]
    in_map = _inproj_column_map()
    q_map, qr_map, k_map, v_map = _mla_weight_maps()
    cos_t, sin_t = rope_tables(positions)
    xc = x.reshape(t, d)
    for l in range(depth):
        w_cat = _relayout_cols(w_in[l], in_map).astype(bf16)
        sc, cf, mc, kr, swq, swk, swv = inproj(xc, w_cat)
        y_sc, y_cf = local_mix(sc, cf, sc_conv_w[l], cf_dw_w[l], cf_dw_b[l], cf_ln_g[l], cf_ln_b[l], batch, seq)
        y_sw = swa(swq, swk, swv, swa_sinks[l], batch, seq)
        mq, mk, mv = mla_prep(mc, kr, cos_t, sin_t, mla_q_norm[l], mla_kv_norm[l],
                              _relayout_cols(mla_w_uq[l], q_map).astype(bf16),
                              _relayout_cols(mla_w_uq[l], qr_map).astype(bf16),
                              _relayout_cols(mla_w_uk[l], k_map).astype(bf16),
                              _relayout_cols(mla_w_uv[l], v_map).astype(bf16))
        y_mla = mla_attn(mq, mk, mv, batch, seq)
        h = outproj(y_sc, y_mla, y_cf, y_sw, xc, mix_norm_g[l], w_out[l].astype(bf16), ln1_g[l], ln1_b[l])
        keys = peer_sub_keys[l].reshape(2 * PEER_HEADS, PEER_NKEYS, PEER_HALF).astype(bf16)
        idx_t, gate_t = peer_route(h, peer_w_q[l].astype(bf16), keys)
        table = peer_table(peer_u, peer_v, l)
        xc = peer_apply(idx_t, gate_t, h, table, ln2_g[l], ln2_b[l])
    return xc.reshape(batch, seq, d)
```
